```python
import math
import jax, jax.numpy as jnp
from jax import lax
import numpy as np

D_MODEL = 4096
BATCH = 1
SEQ = 8192
DEPTH = 2

CTX_LEN = 256
GRID_W = 64
F32 = jnp.float32
EPS = 1e-6
SHORT_CONV = 3

HY_C = 1024
HY_ORDER = 2
HY_BANDS = 16
HY_EMB = 2 * HY_BANDS + 1
HY_FFN = 64
HY_DECAY_TARGET = 1e-2
HY_FAST_DECAY = 0.3
HY_SLOW_DECAY = 1.5

SSD_HEADS = 16
SSD_P = 64
SSD_INNER = SSD_HEADS * SSD_P
SSD_GROUPS = 4
SSD_N = 128
SSD_XBC = SSD_INNER + 2 * SSD_GROUPS * SSD_N
SSD_CHUNK = 128

HG_HEADS = 8
HG_DK = 128
HG_DV = 128
HG_W = HG_HEADS * HG_DV
HG_CHUNK = 64

RET_HEADS = 4
RET_DK = 128
RET_DV = 256
RET_QK = RET_HEADS * RET_DK
RET_W = RET_HEADS * RET_DV
RET_CHUNK = 128
ROPE_BASE = 10000.0

MIX_W = HY_C + SSD_INNER + HG_W + RET_W
IN_SPLITS = ((HY_ORDER + 1) * HY_C, SSD_INNER, SSD_XBC, 2 * SSD_HEADS,
             HG_HEADS * HG_DK, 2 * HG_HEADS * HG_DK, HG_W, HG_W,
             RET_QK, RET_QK, RET_W, RET_W)
IN_W = (HY_ORDER + 1) * HY_C + SSD_INNER + SSD_XBC + 2 * SSD_HEADS + 3 * HG_HEADS * HG_DK + 2 * HG_W + 2 * RET_QK + 2 * RET_W

MOE_GROUPS = 4
MOE_PER_GROUP = 8
MOE_EXPERTS = MOE_GROUPS * MOE_PER_GROUP
MOE_FF = 512
MOE_TOPK = 2

kernel_name = 'hybrid_hyena_ssd_hgrn2_retention_hmoe_dit'


def rmsnorm(x, g):
    xf = x.astype(F32)
    y = xf * lax.rsqrt(jnp.mean(xf * xf, axis=-1, keepdims=True) + EPS)
    return (y * g.astype(F32)).astype(x.dtype)


def group_rmsnorm(x, g, n_groups):
    shp = x.shape
    xg = x.reshape(shp[:-1] + (n_groups, shp[-1] // n_groups))
    xg = xg * lax.rsqrt(jnp.mean(xg * xg, axis=-1, keepdims=True) + EPS)
    return xg.reshape(shp) * g.astype(F32)


def short_conv(x, w, b):
    y = lax.conv_general_dilated(x, w[:, None, :].astype(x.dtype), window_strides=(1,),
                                 padding=[(SHORT_CONV // 2, SHORT_CONV // 2)],
                                 dimension_numbers=('NWC', 'WIO', 'NWC'),
                                 feature_group_count=x.shape[-1])
    return y + b


def _rev(t, d):
    return jnp.flip(t, axis=1) if d else t


def to_cols(t, rows):
    b, n, ch = t.shape
    return jnp.swapaxes(t.reshape(b, rows, GRID_W, ch), 1, 2).reshape(b, n, ch)


def from_cols(t, rows):
    b, n, ch = t.shape
    return jnp.swapaxes(t.reshape(b, GRID_W, rows, ch), 1, 2).reshape(b, n, ch)


def rope(x, pos):
    half = x.shape[-1] // 2
    inv = ROPE_BASE ** (-jnp.arange(half, dtype=F32) / half)
    ang = pos[:, None] * inv[None, :]
    cos = jnp.cos(ang)[None, :, None, :]
    sin = jnp.sin(ang)[None, :, None, :]
    x1, x2 = x[..., :half], x[..., half:]
    return jnp.concatenate([x1 * cos - x2 * sin, x1 * sin + x2 * cos], axis=-1)


def masked_exp(diff, mask):
    return jnp.where(mask, jnp.exp(jnp.where(mask, diff, 0.0)), 0.0)


def chunk_scalar_decay(q, k, v, log_a, s0, chunk):
    q, k, v, log_a = q.astype(F32), k.astype(F32), v.astype(F32), log_a.astype(F32)
    bsz, L, H, dk = q.shape
    dv = v.shape[-1]
    nc = L // chunk
    if s0 is None:
        s0 = jnp.zeros((bsz, H, dk, dv), F32)
    qc = q.reshape(bsz, nc, chunk, H, dk)
    kc = k.reshape(bsz, nc, chunk, H, dk)
    vc = v.reshape(bsz, nc, chunk, H, dv)
    acum = jnp.cumsum(log_a.reshape(bsz, nc, chunk, H), axis=2)
    ah = jnp.moveaxis(acum, 3, 2)
    causal = jnp.tril(jnp.ones((chunk, chunk), bool))
    decay = masked_exp(ah[..., :, None] - ah[..., None, :], causal)
    scores = jnp.einsum('bcihd,bcjhd->bchij', qc, kc) * decay
    y = jnp.einsum('bchij,bcjhe->bcihe', scores, vc)
    a_last = acum[:, :, -1]
    w = jnp.exp(a_last[:, :, None] - acum)
    chunk_kv = jnp.einsum('bcjhd,bcjh,bcjhe->bchde', kc, w, vc)

    def step(s, inp):
        dec, kv = inp
        return dec[..., None, None] * s + kv, s

    s_fin, s_in = lax.scan(step, s0, (jnp.moveaxis(jnp.exp(a_last), 1, 0), jnp.moveaxis(chunk_kv, 1, 0)))
    s_in = jnp.moveaxis(s_in, 0, 1)
    y = y + jnp.einsum('bcihd,bchde,bcih->bcihe', qc, s_in, jnp.exp(acum))
    return y.reshape(bsz, L, H, dv), s_fin


def chunk_vector_decay(q, k, v, log_g, s0, chunk):
    q, k, v, log_g = q.astype(F32), k.astype(F32), v.astype(F32), log_g.astype(F32)
    bsz, L, H, dk = q.shape
    dv = v.shape[-1]
    nc = L // chunk
    if s0 is None:
        s0 = jnp.zeros((bsz, H, dk, dv), F32)

    def resh(t):
        return jnp.moveaxis(t.reshape((bsz, nc, chunk) + t.shape[2:]), 1, 0)

    causal = jnp.tril(jnp.ones((chunk, chunk), bool))[None, :, :, None, None]

    def step(s, inp):
        qc, kc, vc, gc = inp
        gcum = jnp.cumsum(gc, axis=1)
        rel = masked_exp(gcum[:, :, None] - gcum[:, None, :], causal)
        scores = jnp.einsum('bihd,bjhd,bijhd->bhij', qc, kc, rel)
        y = jnp.einsum('bhij,bjhe->bihe', scores, vc) + jnp.einsum('bihd,bhde->bihe', qc * jnp.exp(gcum), s)
        g_last = gcum[:, -1]
        s = jnp.exp(g_last)[..., None] * s + jnp.einsum('bjhd,bjhe->bhde', kc * jnp.exp(g_last[:, None] - gcum), vc)
        return s, y

    s_fin, y = lax.scan(step, s0, (resh(q), resh(k), resh(v), resh(log_g)))
    return jnp.moveaxis(y, 0, 1).reshape(bsz, L, H, dv), s_fin


def hyena_filter_spectrum(L, w1, b1, fr1, w2, b2, fr2, w3):
    t = jnp.linspace(0.0, 1.0, L, dtype=F32)[:, None]
    wpos = 2.0 * math.pi * jnp.arange(L, dtype=F32)[:, None] / L
    f = jnp.linspace(1e-4, HY_BANDS - 1, HY_BANDS, dtype=F32)[None, :]
    feats = jnp.concatenate([t, jnp.cos(f * wpos), -jnp.sin(f * wpos)], axis=-1)
    h = jnp.sin(fr1.astype(F32) * (feats @ w1.astype(F32) + b1.astype(F32)))
    h = jnp.sin(fr2.astype(F32) * (h @ w2.astype(F32) + b2.astype(F32)))
    h = (h @ w3.astype(F32)).reshape(L, HY_ORDER, 2, HY_C)
    deltas = jnp.abs(jnp.linspace(math.log(HY_DECAY_TARGET) / HY_SLOW_DECAY,
                                  math.log(HY_DECAY_TARGET) / HY_FAST_DECAY, HY_C, dtype=F32))
    h = h * jnp.exp(-t * deltas)[:, None, None, :]
    hf, hb = h[:, :, 0], h[:, :, 1]
    kfull = jnp.concatenate([hf, jnp.zeros_like(hf[:1]), hb[:0:-1]], axis=0)
    kfull = kfull / (jnp.sum(jnp.abs(kfull), axis=0, keepdims=True) + EPS)
    return jnp.fft.rfft(kfull, axis=0)


def hyena_operator(u, conv_w, conv_b, filt, bias):
    L = u.shape[1]
    x1, x2, v = jnp.split(short_conv(u, conv_w, conv_b).astype(F32), 3, axis=-1)
    spec = hyena_filter_spectrum(L, *filt)
    bias = bias.astype(F32)

    def long_conv(z, o):
        zf = jnp.fft.rfft(z, n=2 * L, axis=1)
        return jnp.fft.irfft(zf * spec[None, :, o], n=2 * L, axis=1)[:, :L] + z * bias[o]

    z = x1 * long_conv(v, 0)
    return (x2 * long_conv(z, 1)).astype(u.dtype)


def ssd_mixer(z, xbc, dt_raw, conv_w, conv_b, dt_bias, a_log, d_skip, norm_g, init):
    bsz, L, _ = z.shape
    xbc = jax.nn.silu(short_conv(xbc, conv_w, conv_b)).astype(F32)
    xs, bm, cm = jnp.split(xbc, [SSD_INNER, SSD_INNER + SSD_GROUPS * SSD_N], axis=-1)
    xs = xs.reshape(bsz, L, SSD_HEADS, SSD_P)
    rep = SSD_HEADS // SSD_GROUPS
    bm = jnp.repeat(bm.reshape(bsz, L, SSD_GROUPS, SSD_N), rep, axis=2)
    cm = jnp.repeat(cm.reshape(bsz, L, SSD_GROUPS, SSD_N), rep, axis=2)
    dt = jax.nn.softplus(dt_raw.astype(F32).reshape(bsz, L, 2, SSD_HEADS) + dt_bias.astype(F32))
    a = -jnp.exp(a_log.astype(F32))
    y = xs * d_skip.astype(F32)[:, None]
    finals = []
    for d in range(2):
        yd, sd = chunk_scalar_decay(_rev(cm, d), _rev(bm * dt[:, :, d, :, None], d), _rev(xs, d),
                                    _rev(dt[:, :, d] * a[d], d), init[d], SSD_CHUNK)
        y = y + _rev(yd, d)
        finals.append(sd)
    y = y.reshape(bsz, L, SSD_INNER) * jax.nn.silu(z.astype(F32))
    return group_rmsnorm(y, norm_g, SSD_GROUPS).astype(z.dtype), finals


def hgrn2_mixer(q, f_raw, i, g, lb, norm_g, init):
    bsz, L, _ = q.shape
    qh = jax.nn.silu(q.astype(F32)).reshape(bsz, L, HG_HEADS, HG_DK)
    vh = i.astype(F32).reshape(bsz, L, HG_HEADS, HG_DV)
    lbh = lb.reshape(2, HG_HEADS, HG_DK)
    fr = f_raw.astype(F32).reshape(bsz, L, 2, HG_HEADS, HG_DK)
    log_f = jnp.log(lbh + (1.0 - lbh) * jax.nn.sigmoid(fr))
    k = (1.0 - lbh) * jax.nn.sigmoid(-fr)
    outs, finals = [], []
    for d in range(2):
        od, sd = chunk_vector_decay(_rev(qh, d), _rev(k[:, :, d], d), _rev(vh, d),
                                    _rev(log_f[:, :, d], d), init[d], HG_CHUNK)
        outs.append(_rev(od, d))
        finals.append(sd)
    o = (outs[0] + outs[1]).reshape(bsz, L, HG_W)
    o = group_rmsnorm(o, norm_g, HG_HEADS) * jax.nn.silu(g.astype(F32))
    return o.astype(q.dtype), finals


def retention_mixer(q, k, v, g, norm_g, pos, init):
    bsz, L, _ = q.shape
    qh = rope(q.astype(F32).reshape(bsz, L, RET_HEADS, RET_DK), pos)
    kh = rope(k.astype(F32).reshape(bsz, L, RET_HEADS, RET_DK), pos) * (RET_DK ** -0.5)
    vh = v.astype(F32).reshape(bsz, L, RET_HEADS, RET_DV)
    log_gamma = jnp.log1p(-jnp.exp2(-5.0 - jnp.arange(RET_HEADS, dtype=F32)))
    log_a = jnp.broadcast_to(log_gamma, (bsz, L, RET_HEADS))
    outs, finals = [], []
    for d in range(2):
        od, sd = chunk_scalar_decay(_rev(qh, d), _rev(kh, d), _rev(vh, d), log_a, init[d], RET_CHUNK)
        outs.append(_rev(od, d))
        finals.append(sd)
    o = (outs[0] + outs[1]).reshape(bsz, L, RET_W)
    o = group_rmsnorm(o, norm_g, RET_HEADS) * jax.nn.silu(g.astype(F32))
    return o.astype(q.dtype), finals


def token_mixers(h_lat, h_ctx, P, lb, rows, ctx_out):
    offs = [int(o) for o in np.cumsum(IN_SPLITS)[:-1]]
    (hy_c, sz_c, sx_c, sdt_c, hq_c, hf_c, hi_c, hg_c, rq_c, rk_c, rv_c, rg_c) = jnp.split(h_ctx @ P['w_in'], offs, axis=-1)
    (hy_l, sz_l, sx_l, sdt_l, hq_l, hf_l, hi_l, hg_l, rq_l, rk_l, rv_l, rg_l) = jnp.split(h_lat @ P['w_in'], offs, axis=-1)
    m_len, n_len = h_ctx.shape[1], h_lat.shape[1]
    filt = (P['hy_w1'], P['hy_b1'], P['hy_fr1'], P['hy_w2'], P['hy_b2'], P['hy_fr2'], P['hy_w3'])
    a_lat = hyena_operator(hy_l, P['hy_conv_w'], P['hy_conv_b'], filt, P['hy_bias'])
    ssd_p = (P['ssd_conv_w'], P['ssd_conv_b'], P['ssd_dt_bias'], P['ssd_a_log'], P['ssd_d'], P['ssd_norm_g'])
    b_ctx, ssd_states = ssd_mixer(sz_c, sx_c, sdt_c, *ssd_p, (None, None))
    b_lat, _ = ssd_mixer(sz_l, sx_l, sdt_l, *ssd_p, ssd_states)
    c_ctx, hg_states = hgrn2_mixer(hq_c, hf_c, hi_c, hg_c, lb, P['hg_norm_g'], (None, None))
    c_lat, _ = hgrn2_mixer(to_cols(hq_l, rows), to_cols(hf_l, rows), to_cols(hi_l, rows),
                           to_cols(hg_l, rows), lb, P['hg_norm_g'], hg_states)
    c_lat = from_cols(c_lat, rows)
    pos_ctx = jnp.arange(m_len, dtype=F32)
    pos_lat = jnp.arange(n_len, dtype=F32) + m_len
    d_ctx, ret_states = retention_mixer(rq_c, rk_c, rv_c, rg_c, P['ret_norm_g'], pos_ctx, (None, None))
    d_lat, _ = retention_mixer(rq_l, rk_l, rv_l, rg_l, P['ret_norm_g'], pos_lat, ret_states)
    m_lat = jnp.concatenate([a_lat, b_lat, c_lat, d_lat], axis=-1)
    if not ctx_out:
        return m_lat, None
    a_ctx = hyena_operator(hy_c, P['hy_conv_w'], P['hy_conv_b'], filt, P['hy_bias'])
    m_ctx = jnp.concatenate([a_ctx, b_ctx, c_ctx, d_ctx], axis=-1)
    return m_lat, m_ctx


def hier_moe(h, P):
    lg = jnp.einsum('btd,dg->btg', h, P['moe_wg']).astype(F32) + P['moe_bg'].astype(F32)
    pg = jax.nn.softmax(lg, axis=-1)
    gsel = jnp.argmax(lg, axis=-1)
    pg_sel = jnp.max(pg, axis=-1, keepdims=True)
    le = (jnp.einsum('btd,de->bte', h, P['moe_we']).astype(F32) + P['moe_be'].astype(F32))
    le = le.reshape(h.shape[0], h.shape[1], MOE_GROUPS, MOE_PER_GROUP)
    le_sel = jnp.einsum('btge,btg->bte', le, jax.nn.one_hot(gsel, MOE_GROUPS, dtype=F32))
    top_v, top_i = lax.top_k(le_sel, MOE_TOPK)
    top_w = jax.nn.softmax(top_v, axis=-1) * pg_sel
    expert_id = gsel[..., None] * MOE_PER_GROUP + top_i
    combine = jnp.sum(jax.nn.one_hot(expert_id, MOE_EXPERTS, dtype=F32) * top_w[..., None], axis=2)
    a = jnp.einsum('btd,edf->btef', h, P['moe_w1'])
    b = jnp.einsum('btd,edf->btef', h, P['moe_w3'])
    return jnp.einsum('btef,efd,bte->btd', jax.nn.silu(a) * b, P['moe_w2'], combine.astype(h.dtype))


def hybrid_layer(x, ctx, mod_lat, mod_ctx, P, lb, rows, ctx_out):
    sh1, sc1, g1, sh2, sc2, g2 = jnp.split(mod_lat[:, None, :], 6, axis=-1)
    csh1, csc1, cg1, csh2, csc2, cg2 = jnp.split(mod_ctx, 6, axis=-1)
    h_lat = rmsnorm(x, P['norm1']) * (1 + sc1) + sh1
    h_ctx = rmsnorm(ctx, P['norm1']) * (1 + csc1) + csh1
    m_lat, m_ctx = token_mixers(h_lat, h_ctx, P, lb, rows, ctx_out)
    x = x + g1 * (m_lat @ P['w_out'])
    x = x + g2 * hier_moe(rmsnorm(x, P['norm2']) * (1 + sc2) + sh2, P)
    if ctx_out:
        ctx = ctx + cg1 * (m_ctx @ P['w_out'])
        ctx = ctx + cg2 * hier_moe(rmsnorm(ctx, P['norm2']) * (1 + csc2) + csh2, P)
    return x, ctx


def setup_inputs(seed: int = 0) -> dict:
    key = jax.random.key(seed)
    ks = iter(jax.random.split(key, 40))
    D = D_MODEL

    def nrm(shape, scale):
        return jax.random.normal(next(ks), shape, F32) * scale

    def gain(shape):
        return 1.0 + nrm(shape, 0.02)

    inp = {}
    inp['x'] = nrm((BATCH, SEQ, D), 1.0)
    inp['c'] = nrm((BATCH, D), 1.0)
    inp['ctx'] = nrm((BATCH, CTX_LEN, D), 1.0)
    inp['c_ctx'] = nrm((D,), 1.0)
    inp['ada_w'] = nrm((DEPTH, D, 6 * D), 0.5 * D ** -0.5)
    inp['ada_b'] = nrm((DEPTH, 6 * D), 0.02)
    inp['norm1_g'] = gain((DEPTH, D))
    inp['norm2_g'] = gain((DEPTH, D))
    inp['w_in'] = nrm((DEPTH, D, IN_W), D ** -0.5)
    inp['w_out'] = nrm((DEPTH, MIX_W, D), MIX_W ** -0.5)
    inp['hy_conv_w'] = nrm((DEPTH, SHORT_CONV, (HY_ORDER + 1) * HY_C), SHORT_CONV ** -0.5)
    inp['hy_conv_b'] = nrm((DEPTH, (HY_ORDER + 1) * HY_C), 0.02)
    inp['hy_w1'] = nrm((DEPTH, HY_EMB, HY_FFN), HY_EMB ** -0.5)
    inp['hy_b1'] = nrm((DEPTH, HY_FFN), 0.02)
    inp['hy_fr1'] = gain((DEPTH, HY_FFN))
    inp['hy_w2'] = nrm((DEPTH, HY_FFN, HY_FFN), HY_FFN ** -0.5)
    inp['hy_b2'] = nrm((DEPTH, HY_FFN), 0.02)
    inp['hy_fr2'] = gain((DEPTH, HY_FFN))
    inp['hy_w3'] = nrm((DEPTH, HY_FFN, HY_ORDER * 2 * HY_C), HY_FFN ** -0.5)
    inp['hy_bias'] = nrm((DEPTH, HY_ORDER, HY_C), 0.5)
    inp['ssd_conv_w'] = nrm((DEPTH, SHORT_CONV, SSD_XBC), SHORT_CONV ** -0.5)
    inp['ssd_conv_b'] = nrm((DEPTH, SSD_XBC), 0.02)
    dt0 = jnp.exp(jax.random.uniform(next(ks), (DEPTH, 2, SSD_HEADS), F32, math.log(1e-3), math.log(1e-1)))
    inp['ssd_dt_bias'] = dt0 + jnp.log(-jnp.expm1(-dt0))
    inp['ssd_a_log'] = jnp.log(jax.random.uniform(next(ks), (DEPTH, 2, SSD_HEADS), F32, 1.0, 16.0))
    inp['ssd_d'] = gain((DEPTH, SSD_HEADS))
    inp['ssd_norm_g'] = gain((DEPTH, SSD_INNER))
    inp['hg_lb_raw'] = nrm((DEPTH, 2, HG_HEADS * HG_DK), 0.1)
    inp['hg_norm_g'] = gain((DEPTH, HG_W))
    inp['ret_norm_g'] = gain((DEPTH, RET_W))
    inp['moe_wg'] = nrm((DEPTH, D, MOE_GROUPS), D ** -0.5)
    inp['moe_bg'] = nrm((DEPTH, MOE_GROUPS), 0.01)
    inp['moe_we'] = nrm((DEPTH, D, MOE_EXPERTS), D ** -0.5)
    inp['moe_be'] = nrm((DEPTH, MOE_EXPERTS), 0.01)
    inp['moe_w1'] = nrm((DEPTH, MOE_EXPERTS, D, MOE_FF), D ** -0.5)
    inp['moe_w3'] = nrm((DEPTH, MOE_EXPERTS, D, MOE_FF), D ** -0.5)
    inp['moe_w2'] = nrm((DEPTH, MOE_EXPERTS, MOE_FF, D), MOE_FF ** -0.5)
    inp['final_g'] = gain((D,))
    return inp


def reference(x, c, ctx, c_ctx, ada_w, ada_b, norm1_g, norm2_g, w_in, w_out,
              hy_conv_w, hy_conv_b, hy_w1, hy_b1, hy_fr1, hy_w2, hy_b2, hy_fr2, hy_w3, hy_bias,
              ssd_conv_w, ssd_conv_b, ssd_dt_bias, ssd_a_log, ssd_d, ssd_norm_g,
              hg_lb_raw, hg_norm_g, ret_norm_g,
              moe_wg, moe_bg, moe_we, moe_be, moe_w1, moe_w3, moe_w2, final_g):
    rows = x.shape[1] // GRID_W
    lb_prob = jax.nn.softmax(hg_lb_raw.astype(F32), axis=0)
    lb_all = jnp.cumsum(lb_prob, axis=0) - lb_prob[0]
    for l in range(DEPTH):
        P = {
            'norm1': norm1_g[l], 'norm2': norm2_g[l], 'w_in': w_in[l], 'w_out': w_out[l],
            'hy_conv_w': hy_conv_w[l], 'hy_conv_b': hy_conv_b[l], 'hy_w1': hy_w1[l], 'hy_b1': hy_b1[l],
            'hy_fr1': hy_fr1[l], 'hy_w2': hy_w2[l], 'hy_b2': hy_b2[l], 'hy_fr2': hy_fr2[l], 'hy_w3': hy_w3[l],
            'hy_bias': hy_bias[l],
            'ssd_conv_w': ssd_conv_w[l], 'ssd_conv_b': ssd_conv_b[l], 'ssd_dt_bias': ssd_dt_bias[l],
            'ssd_a_log': ssd_a_log[l], 'ssd_d': ssd_d[l], 'ssd_norm_g': ssd_norm_g[l],
            'hg_norm_g': hg_norm_g[l], 'ret_norm_g': ret_norm_g[l],
            'moe_wg': moe_wg[l], 'moe_bg': moe_bg[l], 'moe_we': moe_we[l], 'moe_be': moe_be[l],
            'moe_w1': moe_w1[l], 'moe_w3': moe_w3[l], 'moe_w2': moe_w2[l],
        }
        mod_lat = jax.nn.silu(c) @ ada_w[l] + ada_b[l]
        mod_ctx = jax.nn.silu(c_ctx) @ ada_w[l] + ada_b[l]
        x, ctx = hybrid_layer(x, ctx, mod_lat, mod_ctx, P, lb_all[l], rows, l < DEPTH - 1)
    return rmsnorm(x, final_g)
```

```python
import functools
import math

import numpy as np
import jax
import jax.numpy as jnp
from jax import lax
from jax.experimental import pallas as pl
from jax.experimental.pallas import tpu as pltpu

F32 = jnp.float32
BF16 = jnp.bfloat16
EPS = 1e-6
NEG = -1e30
LANES = 128
VMEM_LIMIT = 56 * 1024 * 1024

D_MODEL = 4096
GRID_W = 64
HY_C = 1024
HY_BANDS = 16
HY_FFN = 64
SSD_HEADS = 16
SSD_P = 64
SSD_INNER = SSD_HEADS * SSD_P
SSD_GROUPS = 4
SSD_N = 128
HG_HEADS = 8
HG_DK = 128
HG_W = HG_HEADS * HG_DK
RET_HEADS = 4
RET_DK = 128
RET_DV = 256
RET_W = RET_HEADS * RET_DV
ROPE_BASE = 10000.0
MOE_GROUPS = 4
MOE_PER_GROUP = 8
MOE_EXPERTS = 32
MOE_FF = 512
MOE_TILE = 256
SCAN_CHUNK = 128
HG_CHUNK = 64
HI = lax.Precision.HIGHEST


def _cp(*sem):
    return pltpu.CompilerParams(dimension_semantics=sem, vmem_limit_bytes=VMEM_LIMIT)


def _silu(x):
    return x * jax.nn.sigmoid(x)


def _split3(x):
    hi = x.astype(BF16)
    r1 = x - hi.astype(F32)
    mid = r1.astype(BF16)
    lo = (r1 - mid.astype(F32)).astype(BF16)
    return hi, mid, lo


def _cumsum_rows(tri, x):
    hi, mid, lo = _split3(x)
    return (jnp.dot(tri, hi, preferred_element_type=F32)
            + jnp.dot(tri, mid, preferred_element_type=F32)
            + jnp.dot(tri, lo, preferred_element_type=F32))


def _ada_kernel(cb_ref, w_ref, b_ref, o_ref, acc_ref, *, nk):
    k = pl.program_id(2)

    @pl.when(k == 0)
    def _():
        acc_ref[...] = jnp.zeros_like(acc_ref)

    tk, tn = w_ref.shape[1], w_ref.shape[2]
    for s in range(2):
        sv = _silu(cb_ref[s])
        for j in range(tn // LANES):
            p = w_ref[0, :, j * LANES:(j + 1) * LANES] * sv
            acc_ref[s, :, j * LANES:(j + 1) * LANES] += p.reshape(tk // 8, 8, LANES).sum(axis=0)

    @pl.when(k == nk - 1)
    def _():
        r0 = acc_ref[0].sum(axis=0, keepdims=True) + b_ref[0]
        r1 = acc_ref[1].sum(axis=0, keepdims=True) + b_ref[0]
        row = lax.broadcasted_iota(jnp.int32, (8, tn), 0)
        o_ref[0] = jnp.where(row == 0, r0, jnp.where(row == 1, r1, 0.0))


def ada_modulation(c, c_ctx, ada_w, ada_b):
    depth, d, n = ada_w.shape
    tk, tn = min(512, d), math.gcd(n, 2048)
    cb = jnp.stack([jnp.broadcast_to(c.reshape(d, 1), (d, LANES)),
                    jnp.broadcast_to(c_ctx.reshape(d, 1), (d, LANES))])
    nk = d // tk
    return pl.pallas_call(
        functools.partial(_ada_kernel, nk=nk),
        grid=(depth, n // tn, nk),
        in_specs=[pl.BlockSpec((2, tk, LANES), lambda l, j, k: (0, k, 0)),
                  pl.BlockSpec((1, tk, tn), lambda l, j, k: (l, k, j)),
                  pl.BlockSpec((1, 1, tn), lambda l, j, k: (l, 0, j))],
        out_specs=pl.BlockSpec((1, 8, tn), lambda l, j, k: (l, 0, j)),
        out_shape=jax.ShapeDtypeStruct((depth, 8, n), F32),
        scratch_shapes=[pltpu.VMEM((2, 8, tn), F32)],
        compiler_params=_cp("arbitrary", "arbitrary", "arbitrary"),
        name="ada_modulation",
    )(cb, ada_w, ada_b.reshape(depth, 1, n))


def _row_select(ref, tm, n_lat):
    row = pl.program_id(0) * tm + lax.broadcasted_iota(jnp.int32, (tm, 1), 0)
    return jnp.where(row < n_lat, ref[0:1, :], ref[1:2, :])


def _norm_kernel(x_ref, g_ref, sc_ref, sh_ref, o_ref, *, n_lat):
    x = x_ref[...]
    tm = x.shape[0]
    y = x * lax.rsqrt(jnp.mean(x * x, axis=-1, keepdims=True) + EPS) * g_ref[...]
    o_ref[...] = (y * (1.0 + _row_select(sc_ref, tm, n_lat)) + _row_select(sh_ref, tm, n_lat)).astype(o_ref.dtype)


def norm_modulate(x, g, sc, sh, n_lat, out_dtype=BF16, tm=256):
    m, d = x.shape
    tm = min(tm, m)
    vec = pl.BlockSpec((1, d), lambda i: (0, 0))
    two = pl.BlockSpec((2, d), lambda i: (0, 0))
    return pl.pallas_call(
        functools.partial(_norm_kernel, n_lat=n_lat),
        grid=(m // tm,),
        in_specs=[pl.BlockSpec((tm, d), lambda i: (i, 0)), vec, two, two],
        out_specs=pl.BlockSpec((tm, d), lambda i: (i, 0)),
        out_shape=jax.ShapeDtypeStruct((m, d), out_dtype),
        compiler_params=_cp("parallel"),
        name="norm_modulate",
    )(x, g.reshape(1, d), sc, sh)


def _mm_kernel(a_ref, w_ref, o_ref):
    o_ref[...] = jnp.dot(a_ref[...], w_ref[...], preferred_element_type=F32).astype(o_ref.dtype)


def matmul(a, w, tm=1024, tn=512, out_dtype=F32):
    m, k = a.shape
    n = w.shape[1]
    tm, tn = min(tm, m), min(tn, n)
    return pl.pallas_call(
        _mm_kernel,
        grid=(m // tm, n // tn),
        in_specs=[pl.BlockSpec((tm, k), lambda i, j: (i, 0)),
                  pl.BlockSpec((k, tn), lambda i, j: (0, j))],
        out_specs=pl.BlockSpec((tm, tn), lambda i, j: (i, j)),
        out_shape=jax.ShapeDtypeStruct((m, n), out_dtype),
        compiler_params=_cp("parallel", "parallel"),
        name="matmul",
    )(a, w)


def _mm_out_kernel(a0_ref, a1_ref, a2_ref, a3_ref, w_ref, r_ref, g_ref, o_ref, *, n_lat):
    tm, kq = a0_ref.shape
    acc = jnp.dot(a0_ref[...], w_ref[0:kq, :], preferred_element_type=F32)
    for q, a_ref in enumerate((a1_ref, a2_ref, a3_ref), start=1):
        acc += jnp.dot(a_ref[...], w_ref[q * kq:(q + 1) * kq, :], preferred_element_type=F32)
    o_ref[...] = r_ref[...] + _row_select(g_ref, tm, n_lat) * acc


def matmul_out(parts, w, res, gate, m, n_lat, tm, tn=512):
    kq = parts[0].shape[1]
    n = w.shape[1]
    tn = min(tn, n)
    a_spec = pl.BlockSpec((tm, kq), lambda i, j: (i, 0))
    return pl.pallas_call(
        functools.partial(_mm_out_kernel, n_lat=n_lat),
        grid=(m // tm, n // tn),
        in_specs=[a_spec, a_spec, a_spec, a_spec,
                  pl.BlockSpec((4 * kq, tn), lambda i, j: (0, j)),
                  pl.BlockSpec((tm, tn), lambda i, j: (i, j)),
                  pl.BlockSpec((2, tn), lambda i, j: (0, j))],
        out_specs=pl.BlockSpec((tm, tn), lambda i, j: (i, j)),
        out_shape=jax.ShapeDtypeStruct((m, n), F32),
        compiler_params=_cp("parallel", "parallel"),
        name="matmul_out",
    )(*parts, w, res, gate)


def _conv3_kernel(x_ref, w_ref, b_ref, o_ref, *, act, starts):
    x = x_ref[...]
    n = x.shape[0]
    row = lax.broadcasted_iota(jnp.int32, x.shape, 0)
    first = row == starts[0]
    last = row == n - 1
    for s in starts[1:]:
        first = first | (row == s)
        last = last | (row == s - 1)
    xm = jnp.where(first, 0.0, pltpu.roll(x, 1, 0))
    xp = jnp.where(last, 0.0, pltpu.roll(x, n - 1, 0))
    y = w_ref[0:1, :] * xm + w_ref[1:2, :] * x + w_ref[2:3, :] * xp + b_ref[...]
    if act:
        y = _silu(y)
    o_ref[...] = y


def conv3(x, w, b, act, starts=(0,), col0=0, tc=LANES):
    n, c = x.shape[0], w.shape[1]
    off = col0 // tc
    return pl.pallas_call(
        functools.partial(_conv3_kernel, act=act, starts=tuple(s for s in starts if s < n)),
        grid=(c // tc,),
        in_specs=[pl.BlockSpec((n, tc), lambda j: (0, j + off)),
                  pl.BlockSpec((3, tc), lambda j: (0, j)),
                  pl.BlockSpec((1, tc), lambda j: (0, j))],
        out_specs=pl.BlockSpec((n, tc), lambda j: (0, j)),
        out_shape=jax.ShapeDtypeStruct((n, c), F32),
        compiler_params=_cp("parallel"),
        name="conv3",
    )(x, w, b.reshape(1, c))


def _hyfilt_kernel(fv_ref, w1_ref, b1_ref, fr1_ref, w2_ref, b2_ref, fr2_ref, w3_ref, dl_ref,
                   k_ref, as_ref, *, seq, nfull, tj):
    i = pl.program_id(0)
    j = i * tj + lax.broadcasted_iota(jnp.int32, (tj, 1), 0)
    valid = (j < seq) | (j > nfull - seq)
    tf = jnp.where(j < seq, j, nfull - j).astype(F32)
    t_lin = tf / (seq - 1)
    wpos = (2.0 * math.pi) * tf / seq
    lane = lax.broadcasted_iota(jnp.int32, (tj, LANES), 1)
    arg = wpos * fv_ref[...]
    feats = jnp.where(lane == 0, t_lin,
                      jnp.where(lane <= HY_BANDS, jnp.cos(arg),
                                jnp.where(lane <= 2 * HY_BANDS, -jnp.sin(arg), 0.0)))
    h = jnp.sin(fr1_ref[...] * (jnp.dot(feats, w1_ref[...], precision=HI, preferred_element_type=F32) + b1_ref[...]))
    h = jnp.sin(fr2_ref[...] * (jnp.dot(h, w2_ref[...], precision=HI, preferred_element_type=F32) + b2_ref[...]))
    h = jnp.dot(h, w3_ref[0], precision=HI, preferred_element_type=F32)
    kv = jnp.where(valid, h * jnp.exp(-t_lin * dl_ref[...]), 0.0)
    k_ref[...] = kv

    @pl.when(i == 0)
    def _():
        as_ref[...] = jnp.zeros_like(as_ref)

    as_ref[0:1, :] += jnp.sum(jnp.abs(kv), axis=0, keepdims=True)


def hyena_filter(seq, nfull, w1, b1, fr1, w2, b2, fr2, w3):
    tj = min(512, nfull)
    nblk = nfull // tj
    f = jnp.linspace(1e-4, HY_BANDS - 1, HY_BANDS, dtype=F32)
    fv = jnp.concatenate([jnp.zeros((1,), F32), f, f, jnp.zeros((LANES - 2 * HY_BANDS - 1,), F32)]).reshape(1, LANES)
    w1p = jnp.pad(w1.astype(F32), ((0, LANES - w1.shape[0]), (0, 0)))
    w3r = w3.astype(F32).reshape(HY_FFN, 2, 2, HY_C).transpose(2, 0, 1, 3).reshape(2, HY_FFN, 2 * HY_C)
    deltas = jnp.abs(jnp.linspace(math.log(1e-2) / 1.5, math.log(1e-2) / 0.3, HY_C, dtype=F32))
    dl = jnp.concatenate([deltas, deltas]).reshape(1, 2 * HY_C)
    small = lambda r, c: pl.BlockSpec((r, c), lambda i: (0, 0))
    return pl.pallas_call(
        functools.partial(_hyfilt_kernel, seq=seq, nfull=nfull, tj=tj),
        grid=(nblk,),
        in_specs=[small(1, LANES), small(LANES, HY_FFN), small(1, HY_FFN), small(1, HY_FFN),
                  small(HY_FFN, HY_FFN), small(1, HY_FFN), small(1, HY_FFN),
                  pl.BlockSpec((1, HY_FFN, 2 * HY_C), lambda i: ((2 * i) // nblk, 0, 0)),
                  small(1, 2 * HY_C)],
        out_specs=[pl.BlockSpec((tj, 2 * HY_C), lambda i: (i, 0)),
                   pl.BlockSpec((8, 2 * HY_C), lambda i: (0, 0))],
        out_shape=[jax.ShapeDtypeStruct((nfull, 2 * HY_C), F32),
                   jax.ShapeDtypeStruct((8, 2 * HY_C), F32)],
        compiler_params=_cp("arbitrary"),
        name="hyena_filter",
    )(fv, w1p, b1.reshape(1, -1).astype(F32), fr1.reshape(1, -1).astype(F32), w2.astype(F32),
      b2.reshape(1, -1).astype(F32), fr2.reshape(1, -1).astype(F32), w3r, dl)


def _dft_tables(n1, rows_in, rows_out):
    nfull = n1 * LANES
    k1 = np.arange(n1)[:, None]
    a = np.arange(n1)[None, :]
    th = 2.0 * np.pi * k1 * a / n1
    frow = np.concatenate([np.cos(th), -np.sin(th)], axis=0)[:, :rows_in]
    b = np.arange(LANES)[None, :]
    tw = 2.0 * np.pi * k1 * b / nfull
    twr, twi = np.cos(tw), -np.sin(tw)
    ph = 2.0 * np.pi * np.arange(LANES)[:, None] * np.arange(LANES)[None, :] / LANES
    fr, fi = np.cos(ph), -np.sin(ph)
    f2 = np.block([[fr, fi], [-fi, fr]])
    f2i = np.block([[fr, -fi], [fi, fr]])
    thi = 2.0 * np.pi * np.arange(rows_out)[:, None] * np.arange(n1)[None, :] / n1
    finv = np.concatenate([np.cos(thi), -np.sin(thi)], axis=1) / nfull
    c = lambda m, dt: jnp.asarray(m, dtype=dt)
    return dict(frow=c(frow, BF16), twr=c(twr, F32), twi=c(twi, F32), f2=c(f2, BF16), f2i=c(f2i, BF16),
                finv=c(finv, BF16))


def _hy_forward(z_bf, frow_ref, twr_ref, twi_ref, pbuf, qbuf, *, cb, n1):
    pbuf[...] = jnp.dot(frow_ref[...], z_bf, preferred_element_type=F32)
    twr, twi = twr_ref[...], twi_ref[...]
    for c in range(cb):
        pr = pbuf[0:n1, c * LANES:(c + 1) * LANES]
        pi = pbuf[n1:2 * n1, c * LANES:(c + 1) * LANES]
        qbuf[c * n1:(c + 1) * n1, 0:LANES] = (pr * twr - pi * twi).astype(BF16)
        qbuf[c * n1:(c + 1) * n1, LANES:2 * LANES] = (pr * twi + pi * twr).astype(BF16)


def _hyspec_kernel(k_ref, sc_ref, frow_ref, twr_ref, twi_ref, f2_ref, o_ref, pbuf, qbuf, *, cb, n1):
    z = (k_ref[...] * sc_ref[...]).astype(BF16)
    _hy_forward(z, frow_ref, twr_ref, twi_ref, pbuf, qbuf, cb=cb, n1=n1)
    o_ref[...] = jnp.dot(qbuf[...], f2_ref[...], preferred_element_type=F32)


def hyena_spectrum(kmat, scale, tabs, n1, cb):
    nch = kmat.shape[1] // LANES
    full = lambda s: pl.BlockSpec(s, lambda i: (0, 0))
    return pl.pallas_call(
        functools.partial(_hyspec_kernel, cb=cb, n1=n1),
        grid=(nch // cb,),
        in_specs=[pl.BlockSpec((n1, cb * LANES), lambda i: (0, i)),
                  pl.BlockSpec((1, cb * LANES), lambda i: (0, i)),
                  full((2 * n1, n1)), full((n1, LANES)), full((n1, LANES)), full((2 * LANES, 2 * LANES))],
        out_specs=pl.BlockSpec((cb * n1, 2 * LANES), lambda i: (i, 0)),
        out_shape=jax.ShapeDtypeStruct((nch * n1, 2 * LANES), F32),
        scratch_shapes=[pltpu.VMEM((2 * n1, cb * LANES), F32), pltpu.VMEM((cb * n1, 2 * LANES), BF16)],
        compiler_params=_cp("parallel"),
        name="hyena_spectrum",
    )(kmat, scale, tabs["frow_full"], tabs["twr"], tabs["twi"], tabs["f2"])


def _hyconv_kernel(z_ref, g_ref, ks_ref, bias_ref, frow_ref, twr_ref, twi_ref, f2_ref, f2i_ref, finv_ref,
                   o_ref, pbuf, qbuf, sbuf, *, cb, n1, sub):
    z = z_ref[...]
    _hy_forward(z.astype(BF16), frow_ref, twr_ref, twi_ref, pbuf, qbuf, cb=cb, n1=n1)
    twr, twi = twr_ref[...], twi_ref[...]
    rows = sub * n1
    for s in range(cb // sub):
        rs = slice(s * rows, (s + 1) * rows)
        zz = jnp.dot(qbuf[rs, :], f2_ref[...], preferred_element_type=F32)
        zr, zi = zz[:, 0:LANES], zz[:, LANES:]
        kr, ki = ks_ref[rs, 0:LANES], ks_ref[rs, LANES:2 * LANES]
        y = jnp.concatenate([zr * kr - zi * ki, zr * ki + zi * kr], axis=1).astype(BF16)
        rr = jnp.dot(y, f2i_ref[...], preferred_element_type=F32)
        for cc in range(sub):
            c = s * sub + cc
            r_re = rr[cc * n1:(cc + 1) * n1, 0:LANES]
            r_im = rr[cc * n1:(cc + 1) * n1, LANES:]
            sbuf[0:n1, c * LANES:(c + 1) * LANES] = (r_re * twr + r_im * twi).astype(BF16)
            sbuf[n1:2 * n1, c * LANES:(c + 1) * LANES] = (r_im * twr - r_re * twi).astype(BF16)
    y = jnp.dot(finv_ref[...], sbuf[...], preferred_element_type=F32)
    o_ref[...] = (y + z * bias_ref[...]) * g_ref[...]


def hyena_conv(z, gate, kspec, order, bias, tabs, n1, cb, sub):
    rows, width = z.shape
    nch = width // LANES
    koff = order * (nch // cb)
    full = lambda s: pl.BlockSpec(s, lambda i: (0, 0))
    io = pl.BlockSpec((rows, cb * LANES), lambda i: (0, i))
    return pl.pallas_call(
        functools.partial(_hyconv_kernel, cb=cb, n1=n1, sub=sub),
        grid=(nch // cb,),
        in_specs=[io, io,
                  pl.BlockSpec((cb * n1, 2 * LANES), lambda i: (i + koff, 0)),
                  pl.BlockSpec((1, cb * LANES), lambda i: (0, i)),
                  full((2 * n1, rows)), full((n1, LANES)), full((n1, LANES)),
                  full((2 * LANES, 2 * LANES)), full((2 * LANES, 2 * LANES)), full((rows, 2 * n1))],
        out_specs=io,
        out_shape=jax.ShapeDtypeStruct((rows, width), F32),
        scratch_shapes=[pltpu.VMEM((2 * n1, cb * LANES), F32), pltpu.VMEM((cb * n1, 2 * LANES), BF16),
                        pltpu.VMEM((2 * n1, cb * LANES), BF16)],
        compiler_params=_cp("parallel"),
        name="hyena_conv",
    )(z, gate, kspec, bias, tabs["frow"], tabs["twr"], tabs["twi"], tabs["f2"], tabs["f2i"], tabs["finv"])


def _to_ab(x, rows):
    n, c = x.shape
    y = x.reshape(n // LANES, LANES, c).transpose(0, 2, 1).reshape(n // LANES, c * LANES)
    return jnp.pad(y, ((0, rows - n // LANES), (0, 0)))


def _from_ab(y, n):
    c = y.shape[1] // LANES
    return y[:n // LANES].reshape(n // LANES, c, LANES).transpose(0, 2, 1).reshape(n, c)


def hyena_mixer(xc, filt, bias):
    seq = xc.shape[0]
    if seq >= 1024:
        n1, rows, cb, sub = 2 * seq // LANES, seq // LANES, 16, 2
    else:
        n1, rows, cb, sub = 16, 16, 128, 16
    nfull = n1 * LANES
    tabs = _dft_tables(n1, rows, rows)
    tabs["frow_full"] = _dft_tables(n1, n1, rows)["frow"]
    kfull, asum = hyena_filter(seq, nfull, *filt)
    scale = jnp.repeat(1.0 / (asum[0] + EPS), LANES).reshape(1, -1)
    kspec = hyena_spectrum(_to_ab(kfull, n1), scale, tabs, n1, cb)
    x1, x2, v = (_to_ab(xc[:, q * HY_C:(q + 1) * HY_C], rows) for q in range(3))
    bias_l = jnp.repeat(bias.astype(F32), LANES, axis=1)
    z = hyena_conv(v, x1, kspec, 0, bias_l[0:1], tabs, n1, cb, sub)
    y = hyena_conv(z, x2, kspec, 1, bias_l[1:2], tabs, n1, cb, sub)
    return _from_ab(y, seq)


def _dscan_kernel(*refs, rev, n_groups, upg, hpu, wd, use_kap):
    if use_kap:
        q_ref, k_ref, v_ref, la_ref, kap_ref, y_ref, st_ref = refs
    else:
        q_ref, k_ref, v_ref, la_ref, y_ref, st_ref = refs
    i = pl.program_id(0)

    @pl.when(i == 0)
    def _():
        st_ref[...] = jnp.zeros_like(st_ref)

    c = q_ref.shape[0]
    row = lax.broadcasted_iota(jnp.int32, (c, c), 0)
    col = lax.broadcasted_iota(jnp.int32, (c, c), 1)
    mask = (row <= col) if rev else (row >= col)
    tri = jnp.where(mask, 1.0, 0.0).astype(BF16)
    cum = _cumsum_rows(tri, la_ref[...])
    cum_t = cum.T
    c_end = cum[0:1, :] if rev else cum[c - 1:c, :]
    e_all = jnp.exp(cum)
    w_all = jnp.exp(c_end - cum)
    e_end = jnp.exp(c_end)
    if use_kap:
        kap = kap_ref[...]
        w_all = w_all * kap
    lane = lax.broadcasted_iota(jnp.int32, (c, wd), 1)
    lane1 = lax.broadcasted_iota(jnp.int32, (1, wd), 1)
    sub = wd // hpu

    def lanesel(cols, ln):
        out = cols[-1]
        for hh in range(hpu - 2, -1, -1):
            out = jnp.where(ln < (hh + 1) * sub, cols[hh], out)
        return out

    for g in range(n_groups):
        qg = q_ref[:, g * LANES:(g + 1) * LANES].astype(BF16)
        kg = k_ref[:, g * LANES:(g + 1) * LANES]
        gm = lax.dot_general(qg, kg.astype(BF16), (((1,), (1,)), ((), ())), preferred_element_type=F32)
        kg_t = kg.T.astype(BF16)
        for uu in range(upg):
            u = g * upg + uu
            v = v_ref[:, u * wd:(u + 1) * wd]
            heads = [u * hpu + hh for hh in range(hpu)]
            st = st_ref[u]
            y = lanesel([e_all[:, h:h + 1] for h in heads], lane) * jnp.dot(
                qg, st.astype(BF16), preferred_element_type=F32)
            for hh, h in enumerate(heads):
                dmat = jnp.exp(jnp.where(mask, cum[:, h:h + 1] - cum_t[h:h + 1, :], NEG))
                vh = v * kap[:, h:h + 1] if use_kap else v
                if hpu > 1:
                    vh = jnp.where((lane >= hh * sub) & (lane < (hh + 1) * sub), vh, 0.0)
                y = y + jnp.dot((gm * dmat).astype(BF16), vh.astype(BF16), preferred_element_type=F32)
            y_ref[:, u * wd:(u + 1) * wd] = y
            vw = v * lanesel([w_all[:, h:h + 1] for h in heads], lane)
            st_ref[u] = (lanesel([e_end[:, h:h + 1] for h in heads], lane1) * st
                         + jnp.dot(kg_t, vw.astype(BF16), preferred_element_type=F32))


def _chunk_order(n, n_lat, c, rev):
    nl, nt = n_lat // c, n // c
    if rev:
        return lambda i: nt - 1 - i
    return lambda i: jnp.where(i < nt - nl, nl + i, i - (nt - nl))


def decay_scan(q, k, v, la, kap, n_lat, *, rev, n_groups, upg, hpu, wd):
    n = q[0].shape[0]
    c = SCAN_CHUNK
    n_units = n_groups * upg
    rmap = _chunk_order(n, n_lat, c, rev)

    def spec(width, blk):
        return pl.BlockSpec((c, width), lambda i: (rmap(i), blk))

    ins = [q, k, v, la] + ([kap] if kap is not None else [])
    widths = [n_groups * LANES, n_groups * LANES, n_units * wd, LANES] + ([LANES] if kap is not None else [])
    return pl.pallas_call(
        functools.partial(_dscan_kernel, rev=rev, n_groups=n_groups, upg=upg, hpu=hpu, wd=wd,
                          use_kap=kap is not None),
        grid=(n // c,),
        in_specs=[spec(w, a[1]) for w, a in zip(widths, ins)],
        out_specs=pl.BlockSpec((c, n_units * wd), lambda i: (rmap(i), 0)),
        out_shape=jax.ShapeDtypeStruct((n, n_units * wd), F32),
        scratch_shapes=[pltpu.VMEM((n_units, LANES, wd), F32)],
        compiler_params=_cp("arbitrary"),
        name="decay_scan_rev" if rev else "decay_scan_fwd",
    )(*[a[0] for a in ins])


def _post_kernel(*refs, n_terms, skip, pre_gate, ngroups):
    terms = refs[:n_terms]
    pos = n_terms
    y = terms[0][...]
    for t in terms[1:]:
        y = y + t[...]
    if skip:
        y = y + refs[pos][...] * refs[pos + 1][...]
        pos += 2
    gate_ref, ng_ref, o_ref = refs[pos:pos + 3]
    gate = _silu(gate_ref[...])
    if pre_gate:
        y = y * gate
    gw = y.shape[1] // ngroups
    for g in range(ngroups):
        sl = slice(g * gw, (g + 1) * gw)
        yg = y[:, sl]
        o = yg * lax.rsqrt(jnp.mean(yg * yg, axis=-1, keepdims=True) + EPS) * ng_ref[:, sl]
        if not pre_gate:
            o = o * gate[:, sl]
        o_ref[:, sl] = o.astype(o_ref.dtype)


def mixer_post(terms, skip, gate, norm_g, *, pre_gate, ngroups, width=1024, tm=256):
    n = terms[0][0].shape[0]
    tm = min(tm, n)
    blk = lambda a: pl.BlockSpec((tm, width), lambda i: (i, a[1]))
    vec = pl.BlockSpec((1, width), lambda i: (0, 0))
    arrays = [t[0] for t in terms]
    specs = [blk(t) for t in terms]
    if skip is not None:
        arrays += [skip[0][0], skip[1].reshape(1, width)]
        specs += [blk(skip[0]), vec]
    arrays += [gate[0], norm_g.reshape(1, width)]
    specs += [blk(gate), vec]
    return pl.pallas_call(
        functools.partial(_post_kernel, n_terms=len(terms), skip=skip is not None, pre_gate=pre_gate,
                          ngroups=ngroups),
        grid=(n // tm,),
        in_specs=specs,
        out_specs=pl.BlockSpec((tm, width), lambda i: (i, 0)),
        out_shape=jax.ShapeDtypeStruct((n, width), BF16),
        compiler_params=_cp("parallel"),
        name="mixer_post",
    )(*arrays)


def _ssd_dt_kernel(x_ref, b_ref, a_ref, dt_ref, la_ref):
    x = x_ref[...] + b_ref[...]
    dt = jnp.maximum(x, 0.0) + jnp.log1p(jnp.exp(-jnp.abs(x)))
    dt_ref[...] = dt
    la_ref[...] = dt * a_ref[...]


def ssd_mixer(u, dt_raw, conv_w, conv_b, dt_bias, a_log, d_skip, norm_g, n_lat):
    n = u.shape[0]
    xbc = conv3(u, conv_w, conv_b, True, starts=(0, n_lat), col0=SSD_INNER)
    pad = lambda t: jnp.pad(t.astype(F32), ((0, 0), (0, LANES - SSD_HEADS))).reshape(1, 2 * LANES)
    tm = 256
    dt, la = pl.pallas_call(
        _ssd_dt_kernel,
        grid=(n // tm,),
        in_specs=[pl.BlockSpec((tm, 2 * LANES), lambda i: (i, 0)),
                  pl.BlockSpec((1, 2 * LANES), lambda i: (0, 0)),
                  pl.BlockSpec((1, 2 * LANES), lambda i: (0, 0))],
        out_specs=[pl.BlockSpec((tm, 2 * LANES), lambda i: (i, 0))] * 2,
        out_shape=[jax.ShapeDtypeStruct((n, 2 * LANES), F32)] * 2,
        compiler_params=_cp("parallel"),
        name="ssd_dt",
    )(dt_raw, pad(dt_bias), pad(-jnp.exp(a_log.astype(F32))))
    ys = [(decay_scan((xbc, 3), (xbc, 2), (xbc, 0), (la, d), (dt, d), n_lat,
                      rev=bool(d), n_groups=SSD_GROUPS, upg=2, hpu=2, wd=LANES), 0) for d in range(2)]
    dsk = jnp.repeat(d_skip.astype(F32), SSD_P)
    return mixer_post(ys, ((xbc, 0), dsk), (u, 0), norm_g, pre_gate=True, ngroups=SSD_GROUPS)


def _rope_kernel(q_ref, k_ref, inv_ref, qo_ref, ko_ref, *, n_lat, n_ctx):
    tm = q_ref.shape[0]
    row = pl.program_id(0) * tm + lax.broadcasted_iota(jnp.int32, (tm, 1), 0)
    pos = jnp.where(row < n_lat, row + n_ctx, row - n_lat).astype(F32)
    ang = pos * inv_ref[...]
    cs, sn = jnp.cos(ang), jnp.sin(ang)
    lane = lax.broadcasted_iota(jnp.int32, (tm, LANES), 1)
    sgn = jnp.where(lane < RET_DK // 2, -sn, sn)
    for h in range(RET_HEADS):
        sl = slice(h * RET_DK, (h + 1) * RET_DK)
        for src, dst, scale in ((q_ref, qo_ref, 1.0), (k_ref, ko_ref, RET_DK ** -0.5)):
            x = src[:, sl]
            y = x * cs + pltpu.roll(x, RET_DK // 2, 1) * sgn
            dst[:, sl] = y * scale


def retention_mixer(u, norm_g, n_lat):
    n = u.shape[0]
    tm = 256
    half = RET_DK // 2
    inv = ROPE_BASE ** (-jnp.arange(half, dtype=F32) / half)
    inv2 = jnp.concatenate([inv, inv]).reshape(1, LANES)
    wq = RET_HEADS * RET_DK
    qr, kr = pl.pallas_call(
        functools.partial(_rope_kernel, n_lat=n_lat, n_ctx=n - n_lat),
        grid=(n // tm,),
        in_specs=[pl.BlockSpec((tm, wq), lambda i: (i, 0)), pl.BlockSpec((tm, wq), lambda i: (i, 1)),
                  pl.BlockSpec((1, LANES), lambda i: (0, 0))],
        out_specs=[pl.BlockSpec((tm, wq), lambda i: (i, 0))] * 2,
        out_shape=[jax.ShapeDtypeStruct((n, wq), F32)] * 2,
        compiler_params=_cp("parallel"),
        name="rope",
    )(u, u, inv2)
    log_gamma = jnp.log1p(-jnp.exp2(-5.0 - jnp.arange(RET_HEADS, dtype=F32)))
    la = jnp.broadcast_to(jnp.pad(log_gamma, (0, LANES - RET_HEADS)).reshape(1, LANES), (n, LANES))
    ys = [(decay_scan((qr, 0), (kr, 0), (u, 1), (la, 0), None, n_lat,
                      rev=bool(d), n_groups=RET_HEADS, upg=1, hpu=1, wd=RET_DV), 0) for d in range(2)]
    return mixer_post(ys, None, (u, 2), norm_g, pre_gate=False, ngroups=RET_HEADS)


def _anchor_rows(gc, b, rev):
    c, n = gc.shape
    shift = b if rev else b - 1
    if 2 * b >= 8:
        return jnp.concatenate([jnp.broadcast_to(gc[s + shift:s + shift + 1, :], (2 * b, n))
                                for s in range(0, c, 2 * b)], axis=0)
    sub = lax.broadcasted_iota(jnp.int32, (8, n), 0)
    tiles = []
    for t in range(0, c, 8):
        tile = None
        for m in range(8 // (2 * b)):
            r = t + m * 2 * b + shift
            cand = jnp.broadcast_to(gc[r:r + 1, :], (8, n))
            tile = cand if tile is None else jnp.where(sub >= m * 2 * b, cand, tile)
        tiles.append(tile)
    return jnp.concatenate(tiles, axis=0)


def _hg_kernel(q_ref, f_ref, v_ref, lb_ref, y_ref, st_ref, *, rev):
    i = pl.program_id(0)

    @pl.when(i == 0)
    def _():
        st_ref[...] = jnp.zeros_like(st_ref)

    c, width = q_ref.shape
    nh = width // HG_DK
    lb = lb_ref[...]
    qh = _silu(q_ref[...])
    fr = f_ref[...]
    lg = jnp.log(lb + (1.0 - lb) * jax.nn.sigmoid(fr))
    kk = (1.0 - lb) * jax.nn.sigmoid(-fr)
    row = lax.broadcasted_iota(jnp.int32, (c, c), 0)
    col = lax.broadcasted_iota(jnp.int32, (c, c), 1)
    tri = jnp.where((row <= col) if rev else (row >= col), 1.0, 0.0).astype(BF16)
    gc = _cumsum_rows(tri, lg)
    g_end = gc[0:1, :] if rev else gc[c - 1:c, :]
    qe = (qh * jnp.exp(gc)).astype(BF16)
    kw = (kk * jnp.exp(g_end - gc)).astype(BF16)
    dec = jnp.exp(g_end)
    vb = v_ref[...].astype(BF16)
    qb, kb = qh.astype(BF16), kk.astype(BF16)
    nt = (((1,), (1,)), ((), ()))
    hs = [slice(h * HG_DK, (h + 1) * HG_DK) for h in range(nh)]
    scores = [jnp.where(row == col, lax.dot_general(qb[:, s], kb[:, s], nt, preferred_element_type=F32), 0.0)
              for s in hs]
    ridx = lax.broadcasted_iota(jnp.int32, (c, 1), 0)
    lvl = 0
    while (1 << lvl) < c:
        b = 1 << lvl
        anchor = _anchor_rows(gc, b, rev)
        upper = ((ridx >> lvl) & 1) == 1
        q_on, k_on = (~upper, upper) if rev else (upper, ~upper)
        qt = (qh * jnp.exp(jnp.where(q_on, gc - anchor, NEG))).astype(BF16)
        kt = (kk * jnp.exp(jnp.where(k_on, anchor - gc, NEG))).astype(BF16)
        same = (row >> (lvl + 1)) == (col >> (lvl + 1))
        for h, s in enumerate(hs):
            sc = lax.dot_general(qt[:, s], kt[:, s], nt, preferred_element_type=F32)
            scores[h] = scores[h] + jnp.where(same, sc, 0.0)
        lvl += 1
    for h, s in enumerate(hs):
        st = st_ref[h]
        y = jnp.dot(scores[h].astype(BF16), vb[:, s], preferred_element_type=F32)
        y = y + lax.dot_general(qe[:, s], st.astype(BF16), nt, preferred_element_type=F32)
        y_ref[:, s] = y
        st_ref[h] = dec[:, s] * st + lax.dot_general(vb[:, s], kw[:, s], (((0,), (0,)), ((), ())),
                                                     preferred_element_type=F32)


def hgrn2_scan(u, lb, n_lat, *, rev, d):
    n = u.shape[0]
    c = HG_CHUNK
    rmap = _chunk_order(n, n_lat, c, rev)
    spec = lambda blk: pl.BlockSpec((c, HG_W), lambda i: (rmap(i), blk))
    return pl.pallas_call(
        functools.partial(_hg_kernel, rev=rev),
        grid=(n // c,),
        in_specs=[spec(0), spec(1 + d), spec(3), pl.BlockSpec((1, HG_W), lambda i: (0, 0))],
        out_specs=pl.BlockSpec((c, HG_W), lambda i: (rmap(i), 0)),
        out_shape=jax.ShapeDtypeStruct((n, HG_W), F32),
        scratch_shapes=[pltpu.VMEM((HG_HEADS, HG_DK, HG_DK), F32)],
        compiler_params=_cp("arbitrary"),
        name="hgrn2_scan_rev" if rev else "hgrn2_scan_fwd",
    )(u, u, u, lb)


def hgrn2_mixer(u, lb, norm_g, n_lat):
    ys = [(hgrn2_scan(u, lb[d:d + 1].astype(F32), n_lat, rev=bool(d), d=d), 0) for d in range(2)]
    return mixer_post(ys, None, (u, 4), norm_g, pre_gate=False, ngroups=HG_HEADS)


def _router_kernel(x_ref, g_ref, sc_ref, sh_ref, wr_ref, br_ref, h_ref, r_ref, *, n_lat):
    x = x_ref[...]
    tm = x.shape[0]
    y = x * lax.rsqrt(jnp.mean(x * x, axis=-1, keepdims=True) + EPS) * g_ref[...]
    h = y * (1.0 + _row_select(sc_ref, tm, n_lat)) + _row_select(sh_ref, tm, n_lat)
    h_ref[...] = h
    lg = jnp.dot(h, wr_ref[...], precision=HI, preferred_element_type=F32) + br_ref[...]
    lane = lax.broadcasted_iota(jnp.int32, lg.shape, 1).astype(F32)
    big = float(LANES)
    is_g = lane < MOE_GROUPS
    gmax = jnp.max(jnp.where(is_g, lg, -jnp.inf), axis=1, keepdims=True)
    gsel = jnp.min(jnp.where(is_g & (lg == gmax), lane, big), axis=1, keepdims=True)
    pg = 1.0 / jnp.sum(jnp.where(is_g, jnp.exp(lg - gmax), 0.0), axis=1, keepdims=True)
    lo = MOE_GROUPS + MOE_PER_GROUP * gsel
    in_e = (lane >= lo) & (lane < lo + MOE_PER_GROUP)
    v1 = jnp.max(jnp.where(in_e, lg, -jnp.inf), axis=1, keepdims=True)
    i1 = jnp.min(jnp.where(in_e & (lg == v1), lane, big), axis=1, keepdims=True)
    rest = in_e & (lane != i1)
    v2 = jnp.max(jnp.where(rest, lg, -jnp.inf), axis=1, keepdims=True)
    i2 = jnp.min(jnp.where(rest & (lg == v2), lane, big), axis=1, keepdims=True)
    t = jnp.exp(v2 - v1)
    w1 = pg / (1.0 + t)
    w2 = pg * t / (1.0 + t)
    r_ref[...] = jnp.where(lane == 0, i1 - MOE_GROUPS,
                           jnp.where(lane == 1, i2 - MOE_GROUPS,
                                     jnp.where(lane == 2, w1, jnp.where(lane == 3, w2, 0.0))))


def moe_router(x, g, sc, sh, wr, br, m, n_lat, tm=256):
    d = x.shape[1]
    vec = pl.BlockSpec((1, d), lambda i: (0, 0))
    two = pl.BlockSpec((2, d), lambda i: (0, 0))
    return pl.pallas_call(
        functools.partial(_router_kernel, n_lat=n_lat),
        grid=(m // tm,),
        in_specs=[pl.BlockSpec((tm, d), lambda i: (i, 0)), vec, two, two,
                  pl.BlockSpec((d, LANES), lambda i: (0, 0)), pl.BlockSpec((1, LANES), lambda i: (0, 0))],
        out_specs=[pl.BlockSpec((tm, d), lambda i: (i, 0)), pl.BlockSpec((tm, LANES), lambda i: (i, 0))],
        out_shape=[jax.ShapeDtypeStruct((m, d), F32), jax.ShapeDtypeStruct((m, LANES), F32)],
        compiler_params=_cp("parallel"),
        name="moe_router",
    )(x, g.reshape(1, d), sc, sh, wr, br)


def _moe_plan(route, m, n_tiles):
    t = MOE_TILE
    pe = jnp.concatenate([route[:, 0], route[:, 1]]).astype(jnp.int32)
    onehot = (pe[:, None] == jnp.arange(MOE_EXPERTS, dtype=jnp.int32)[None, :]).astype(jnp.int32)
    csum = jnp.cumsum(onehot, axis=0)
    rank = jnp.take_along_axis(csum, pe[:, None], axis=1)[:, 0] - 1
    cnt = csum[-1]
    ntile = (cnt + t - 1) // t
    tend = jnp.cumsum(ntile)
    tstart = tend - ntile
    slot = tstart[pe] * t + rank
    tid = jnp.arange(n_tiles, dtype=jnp.int32)
    tile_e = jnp.clip(jnp.searchsorted(tend, tid, side="right"), 0, MOE_EXPERTS - 1).astype(jnp.int32)
    tile_valid = jnp.where(tid < tend[-1], jnp.clip(cnt[tile_e] - (tid - tstart[tile_e]) * t, 0, t), 0)
    tok = jnp.tile(jnp.arange(m, dtype=jnp.int32), 2)
    row_tok = jnp.zeros((n_tiles * t,), jnp.int32).at[slot].set(tok)
    return row_tok, tile_e, tile_valid.astype(jnp.int32), slot[:m], slot[m:]


def _row_copies(idx_ref, base, src_hbm, dst, sem, rows, wait):
    def body(r, carry):
        src_row = 0 if wait else idx_ref[base + r]
        cp = pltpu.make_async_copy(src_hbm.at[pl.ds(src_row, 1), :], dst.at[pl.ds(r, 1), :], sem)
        if wait:
            cp.wait()
        else:
            cp.start()
        return carry
    lax.fori_loop(0, rows, body, 0)


def _gather_kernel(tok_ref, h_hbm, o_ref, buf, sem):
    t = o_ref.shape[0]
    base = pl.program_id(0) * t
    _row_copies(tok_ref, base, h_hbm, buf, sem.at[0], t, wait=False)
    _row_copies(tok_ref, base, h_hbm, buf, sem.at[0], t, wait=True)
    o_ref[...] = buf[...].astype(o_ref.dtype)


def moe_gather(h, row_tok, n_tiles):
    t, d = MOE_TILE, h.shape[1]
    return pl.pallas_call(
        _gather_kernel,
        grid_spec=pltpu.PrefetchScalarGridSpec(
            num_scalar_prefetch=1, grid=(n_tiles,),
            in_specs=[pl.BlockSpec(memory_space=pl.ANY)],
            out_specs=pl.BlockSpec((t, d), lambda i, tok: (i, 0)),
            scratch_shapes=[pltpu.VMEM((t, d), F32), pltpu.SemaphoreType.DMA((1,))]),
        out_shape=jax.ShapeDtypeStruct((n_tiles * t, d), BF16),
        compiler_params=_cp("arbitrary"),
        name="moe_gather",
    )(row_tok, h)


def _expert_kernel(te_ref, tv_ref, x_ref, w1_ref, w3_ref, w2_ref, o_ref):
    i = pl.program_id(0)

    @pl.when(tv_ref[i] > 0)
    def _():
        x = x_ref[...]
        a = jnp.dot(x, w1_ref[0], preferred_element_type=F32)
        b = jnp.dot(x, w3_ref[0], preferred_element_type=F32)
        o_ref[...] = jnp.dot((_silu(a) * b).astype(BF16), w2_ref[0], preferred_element_type=F32)

    @pl.when(tv_ref[i] == 0)
    def _():
        o_ref[...] = jnp.zeros_like(o_ref)


def moe_experts(xs, tile_e, tile_valid, w1, w3, w2):
    t, d = MOE_TILE, xs.shape[1]
    n_tiles = xs.shape[0] // t
    ff = w1.shape[2]
    return pl.pallas_call(
        _expert_kernel,
        grid_spec=pltpu.PrefetchScalarGridSpec(
            num_scalar_prefetch=2, grid=(n_tiles,),
            in_specs=[pl.BlockSpec((t, d), lambda i, te, tv: (i, 0)),
                      pl.BlockSpec((1, d, ff), lambda i, te, tv: (te[i], 0, 0)),
                      pl.BlockSpec((1, d, ff), lambda i, te, tv: (te[i], 0, 0)),
                      pl.BlockSpec((1, ff, d), lambda i, te, tv: (te[i], 0, 0))],
            out_specs=pl.BlockSpec((t, d), lambda i, te, tv: (i, 0))),
        out_shape=jax.ShapeDtypeStruct((n_tiles * t, d), F32),
        compiler_params=_cp("arbitrary"),
        name="moe_experts",
    )(tile_e, tile_valid, xs, w1, w3, w2)


def _combine_kernel(s0_ref, s1_ref, x_ref, r_ref, g_ref, fg_ref, ys_hbm, o_ref, buf, sem, *, n_lat, final):
    tt = x_ref.shape[0]
    base = pl.program_id(0) * tt
    for k, s_ref in enumerate((s0_ref, s1_ref)):
        _row_copies(s_ref, base, ys_hbm, buf.at[k], sem.at[k], tt, wait=False)
    for k, s_ref in enumerate((s0_ref, s1_ref)):
        _row_copies(s_ref, base, ys_hbm, buf.at[k], sem.at[k], tt, wait=True)
    r = r_ref[...]
    y = x_ref[...] + _row_select(g_ref, tt, n_lat) * (r[:, 2:3] * buf[0] + r[:, 3:4] * buf[1])
    if final:
        y = y * lax.rsqrt(jnp.mean(y * y, axis=-1, keepdims=True) + EPS) * fg_ref[...]
    o_ref[...] = y


def moe_combine(x, route, gate, final_g, ys, slot0, slot1, m, n_lat, final, tt=256):
    d = x.shape[1]
    return pl.pallas_call(
        functools.partial(_combine_kernel, n_lat=n_lat, final=final),
        grid_spec=pltpu.PrefetchScalarGridSpec(
            num_scalar_prefetch=2, grid=(m // tt,),
            in_specs=[pl.BlockSpec((tt, d), lambda i, a, b: (i, 0)),
                      pl.BlockSpec((tt, LANES), lambda i, a, b: (i, 0)),
                      pl.BlockSpec((2, d), lambda i, a, b: (0, 0)),
                      pl.BlockSpec((1, d), lambda i, a, b: (0, 0)),
                      pl.BlockSpec(memory_space=pl.ANY)],
            out_specs=pl.BlockSpec((tt, d), lambda i, a, b: (i, 0)),
            scratch_shapes=[pltpu.VMEM((2, tt, d), F32), pltpu.SemaphoreType.DMA((2,))]),
        out_shape=jax.ShapeDtypeStruct((m, d), F32),
        compiler_params=_cp("arbitrary"),
        name="moe_combine",
    )(slot0, slot1, x, route, gate, final_g.reshape(1, d), ys)


def hier_moe(x, norm_g, sc, sh, gate, P, final_g, m, n_lat, final):
    n_tiles = 2 * m // MOE_TILE + MOE_EXPERTS
    h, route = moe_router(x, norm_g, sc, sh, P["moe_wr"], P["moe_br"], m, n_lat)
    row_tok, tile_e, tile_valid, slot0, slot1 = _moe_plan(route, m, n_tiles)
    xs = moe_gather(h, row_tok, n_tiles)
    ys = moe_experts(xs, tile_e, tile_valid, P["moe_w1"], P["moe_w3"], P["moe_w2"])
    return moe_combine(x, route, gate, final_g, ys, slot0, slot1, m, n_lat, final)


def _to_cols(t, rows):
    n, ch = t.shape
    return jnp.swapaxes(t.reshape(rows, GRID_W, ch), 0, 1).reshape(n, ch)


def _from_cols(t, rows):
    n, ch = t.shape
    return jnp.swapaxes(t.reshape(GRID_W, rows, ch), 0, 1).reshape(n, ch)


def _layer(xa, mod, P, lb, n_lat, last, final_g):
    n, d = xa.shape
    sh1, sc1, g1, sh2, sc2, g2 = (mod[:, q * d:(q + 1) * d] for q in range(6))
    grid_rows = n_lat // GRID_W
    h = norm_modulate(xa, P["norm1"], sc1, sh1, n_lat)
    h_cols = jnp.concatenate([_to_cols(h[:n_lat], grid_rows), h[n_lat:]], axis=0)
    tm = 768 if n % 768 == 0 else 256
    u_hy = matmul(h, P["w_hy"], tm=tm)
    u_ssd = matmul(h, P["w_ssd"], tm=tm)
    dt_raw = matmul(h, P["w_dt"], tm=tm, tn=256)
    u_hg = matmul(h_cols, P["w_hg"], tm=tm)
    u_ret = matmul(h, P["w_ret"], tm=tm)
    filt = (P["hy_w1"], P["hy_b1"], P["hy_fr1"], P["hy_w2"], P["hy_b2"], P["hy_fr2"], P["hy_w3"])
    xc = conv3(u_hy, P["hy_conv_w"], P["hy_conv_b"], False, starts=(0, n_lat))
    a = hyena_mixer(xc[:n_lat], filt, P["hy_bias"]).astype(BF16)
    if not last:
        a = jnp.concatenate([a, hyena_mixer(xc[n_lat:], filt, P["hy_bias"]).astype(BF16)], axis=0)
    b = ssd_mixer(u_ssd, dt_raw, P["ssd_conv_w"], P["ssd_conv_b"], P["ssd_dt_bias"], P["ssd_a_log"],
                  P["ssd_d"], P["ssd_norm_g"], n_lat)
    c = hgrn2_mixer(u_hg, lb, P["hg_norm_g"], n_lat)
    c = jnp.concatenate([_from_cols(c[:n_lat], grid_rows), c[n_lat:]], axis=0)
    dd = retention_mixer(u_ret, P["ret_norm_g"], n_lat)
    m = n_lat if last else n
    xa = matmul_out((a, b, c, dd), P["w_out"], xa, g1, m, n_lat, tm=1024 if m % 1024 == 0 else tm)
    return hier_moe(xa, P["norm2"], sc2, sh2, g2, P, final_g, m, n_lat, last)


def kernel(x, c, ctx, c_ctx, ada_w, ada_b, norm1_g, norm2_g, w_in, w_out, hy_conv_w, hy_conv_b, hy_w1, hy_b1, hy_fr1, hy_w2, hy_b2, hy_fr2, hy_w3, hy_bias, ssd_conv_w, ssd_conv_b, ssd_dt_bias, ssd_a_log, ssd_d, ssd_norm_g, hg_lb_raw, hg_norm_g, ret_norm_g, moe_wg, moe_bg, moe_we, moe_be, moe_w1, moe_w3, moe_w2, final_g):
    depth = ada_w.shape[0]
    n_lat = x.shape[1]
    lb_prob = jax.nn.softmax(hg_lb_raw.astype(F32), axis=0)
    lb_all = jnp.cumsum(lb_prob, axis=0) - lb_prob[0]
    mods = ada_modulation(c, c_ctx, ada_w, ada_b)
    xa = jnp.concatenate([x[0], ctx[0]], axis=0)
    o_hy, o_ssd, o_dt, o_hg, o_ret = 0, 3072, 6144, 6176, 11296
    for l in range(depth):
        wl = w_in[l]
        w_dt = jnp.zeros((wl.shape[0], 2 * LANES), wl.dtype)
        w_dt = w_dt.at[:, 0:SSD_HEADS].set(wl[:, o_dt:o_dt + SSD_HEADS])
        w_dt = w_dt.at[:, LANES:LANES + SSD_HEADS].set(wl[:, o_dt + SSD_HEADS:o_dt + 2 * SSD_HEADS])
        pad_r = LANES - MOE_GROUPS - MOE_EXPERTS
        P = {
            "norm1": norm1_g[l], "norm2": norm2_g[l],
            "w_hy": wl[:, o_hy:o_ssd].astype(BF16), "w_ssd": wl[:, o_ssd:o_dt].astype(BF16),
            "w_dt": w_dt.astype(BF16), "w_hg": wl[:, o_hg:o_ret].astype(BF16), "w_ret": wl[:, o_ret:].astype(BF16),
            "w_out": w_out[l].astype(BF16),
            "hy_conv_w": hy_conv_w[l], "hy_conv_b": hy_conv_b[l], "hy_w1": hy_w1[l], "hy_b1": hy_b1[l],
            "hy_fr1": hy_fr1[l], "hy_w2": hy_w2[l], "hy_b2": hy_b2[l], "hy_fr2": hy_fr2[l], "hy_w3": hy_w3[l],
            "hy_bias": hy_bias[l],
            "ssd_conv_w": ssd_conv_w[l], "ssd_conv_b": ssd_conv_b[l], "ssd_dt_bias": ssd_dt_bias[l],
            "ssd_a_log": ssd_a_log[l], "ssd_d": ssd_d[l], "ssd_norm_g": ssd_norm_g[l],
            "hg_norm_g": hg_norm_g[l], "ret_norm_g": ret_norm_g[l],
            "moe_wr": jnp.pad(jnp.concatenate([moe_wg[l], moe_we[l]], axis=1).astype(F32), ((0, 0), (0, pad_r))),
            "moe_br": jnp.pad(jnp.concatenate([moe_bg[l], moe_be[l]]).astype(F32), (0, pad_r)).reshape(1, LANES),
            "moe_w1": moe_w1[l].astype(BF16), "moe_w3": moe_w3[l].astype(BF16), "moe_w2": moe_w2[l].astype(BF16),
        }
        xa = _layer(xa, mods[l, 0:2], P, lb_all[l], n_lat, l == depth - 1, final_g)
    return xa[None]
```

```python
import functools
import math

import numpy as np
import jax
import jax.numpy as jnp
from jax import lax
from jax.experimental import pallas as pl
from jax.experimental.pallas import tpu as pltpu

F32 = jnp.float32
BF16 = jnp.bfloat16
EPS = 1e-6
NEG = -1e30
LANES = 128
VMEM_LIMIT = 56 * 1024 * 1024

D_MODEL = 4096
GRID_W = 64
HY_C = 1024
HY_BANDS = 16
HY_FFN = 64
SSD_HEADS = 16
SSD_P = 64
SSD_INNER = SSD_HEADS * SSD_P
SSD_GROUPS = 4
SSD_N = 128
HG_HEADS = 8
HG_DK = 128
HG_W = HG_HEADS * HG_DK
RET_HEADS = 4
RET_DK = 128
RET_DV = 256
RET_W = RET_HEADS * RET_DV
ROPE_BASE = 10000.0
MOE_GROUPS = 4
MOE_PER_GROUP = 8
MOE_EXPERTS = 32
MOE_FF = 512
MOE_TILE = 256
SCAN_CHUNK = 128
HG_CHUNK = 64
HI = lax.Precision.HIGHEST
W_IN_HY, W_IN_SSD, W_IN_HG, W_IN_RET, W_IN_DT = 0, 3072, 6144, 11264, 14336


def _cp(*sem):
    return pltpu.CompilerParams(dimension_semantics=sem, vmem_limit_bytes=VMEM_LIMIT)


def _silu(x):
    return x * jax.nn.sigmoid(x)


def _split3(x):
    hi = x.astype(BF16)
    r1 = x - hi.astype(F32)
    mid = r1.astype(BF16)
    lo = (r1 - mid.astype(F32)).astype(BF16)
    return hi, mid, lo


def _cumsum_rows(tri, x):
    hi, mid, lo = _split3(x)
    return (jnp.dot(tri, hi, preferred_element_type=F32)
            + jnp.dot(tri, mid, preferred_element_type=F32)
            + jnp.dot(tri, lo, preferred_element_type=F32))


def _ada_kernel(cb_ref, w_ref, b_ref, o_ref, acc_ref, *, nk):
    k = pl.program_id(2)

    @pl.when(k == 0)
    def _():
        acc_ref[...] = jnp.zeros_like(acc_ref)

    tk, tn = w_ref.shape[1], w_ref.shape[2]
    for s in range(2):
        sv = _silu(cb_ref[s])
        for j in range(tn // LANES):
            p = w_ref[0, :, j * LANES:(j + 1) * LANES] * sv
            acc_ref[s, :, j * LANES:(j + 1) * LANES] += p.reshape(tk // 8, 8, LANES).sum(axis=0)

    @pl.when(k == nk - 1)
    def _():
        r0 = acc_ref[0].sum(axis=0, keepdims=True) + b_ref[0]
        r1 = acc_ref[1].sum(axis=0, keepdims=True) + b_ref[0]
        row = lax.broadcasted_iota(jnp.int32, (8, tn), 0)
        o_ref[0] = jnp.where(row == 0, r0, jnp.where(row == 1, r1, 0.0))


def ada_modulation(c, c_ctx, ada_w, ada_b):
    depth, d, n = ada_w.shape
    tk, tn = min(512, d), math.gcd(n, 2048)
    cb = jnp.stack([jnp.broadcast_to(c.reshape(d, 1), (d, LANES)),
                    jnp.broadcast_to(c_ctx.reshape(d, 1), (d, LANES))])
    nk = d // tk
    return pl.pallas_call(
        functools.partial(_ada_kernel, nk=nk),
        grid=(depth, n // tn, nk),
        in_specs=[pl.BlockSpec((2, tk, LANES), lambda l, j, k: (0, k, 0)),
                  pl.BlockSpec((1, tk, tn), lambda l, j, k: (l, k, j)),
                  pl.BlockSpec((1, 1, tn), lambda l, j, k: (l, 0, j))],
        out_specs=pl.BlockSpec((1, 8, tn), lambda l, j, k: (l, 0, j)),
        out_shape=jax.ShapeDtypeStruct((depth, 8, n), F32),
        scratch_shapes=[pltpu.VMEM((2, 8, tn), F32)],
        compiler_params=_cp("arbitrary", "arbitrary", "arbitrary"),
        name="ada_modulation",
    )(cb, ada_w, ada_b.reshape(depth, 1, n))


def _row_select(ref, tm, n_lat):
    row = pl.program_id(0) * tm + lax.broadcasted_iota(jnp.int32, (tm, 1), 0)
    return jnp.where(row < n_lat, ref[0:1, :], ref[1:2, :])


def _norm_kernel(x_ref, g_ref, sc_ref, sh_ref, o_ref, *, n_lat):
    x = x_ref[...]
    tm = x.shape[0]
    y = x * lax.rsqrt(jnp.mean(x * x, axis=-1, keepdims=True) + EPS) * g_ref[...]
    o_ref[...] = (y * (1.0 + _row_select(sc_ref, tm, n_lat)) + _row_select(sh_ref, tm, n_lat)).astype(o_ref.dtype)


def norm_modulate(x, g, sc, sh, n_lat, out_dtype=BF16, tm=256):
    m, d = x.shape
    tm = min(tm, m)
    vec = pl.BlockSpec((1, d), lambda i: (0, 0))
    two = pl.BlockSpec((2, d), lambda i: (0, 0))
    return pl.pallas_call(
        functools.partial(_norm_kernel, n_lat=n_lat),
        grid=(m // tm,),
        in_specs=[pl.BlockSpec((tm, d), lambda i: (i, 0)), vec, two, two],
        out_specs=pl.BlockSpec((tm, d), lambda i: (i, 0)),
        out_shape=jax.ShapeDtypeStruct((m, d), out_dtype),
        compiler_params=_cp("parallel"),
        name="norm_modulate",
    )(x, g.reshape(1, d), sc, sh)


def _mm_kernel(a_ref, w_ref, o_ref):
    o_ref[...] = jnp.dot(a_ref[...], w_ref[0], preferred_element_type=F32).astype(o_ref.dtype)


def matmul(a, w, layer, col0, n, tm, tn=512, out_dtype=F32):
    m, k = a.shape
    tn = min(tn, n)
    off = col0 // tn
    return pl.pallas_call(
        _mm_kernel,
        grid=(m // tm, n // tn),
        in_specs=[pl.BlockSpec((tm, k), lambda i, j: (i, 0)),
                  pl.BlockSpec((1, k, tn), lambda i, j: (layer, 0, j + off))],
        out_specs=pl.BlockSpec((tm, tn), lambda i, j: (i, j)),
        out_shape=jax.ShapeDtypeStruct((m, n), out_dtype),
        compiler_params=_cp("parallel", "parallel"),
        name="matmul",
    )(a, w)


def _mm_out_kernel(a0_ref, a1_ref, a2_ref, a3_ref, w_ref, r_ref, g_ref, o_ref, *, n_lat):
    tm, kq = a0_ref.shape
    acc = jnp.dot(a0_ref[...], w_ref[0, 0:kq, :], preferred_element_type=F32)
    for q, a_ref in enumerate((a1_ref, a2_ref, a3_ref), start=1):
        acc += jnp.dot(a_ref[...], w_ref[0, q * kq:(q + 1) * kq, :], preferred_element_type=F32)
    o_ref[...] = r_ref[...] + _row_select(g_ref, tm, n_lat) * acc


def matmul_out(parts, w, layer, res, gate, m, n_lat, tm, tn=512):
    kq = parts[0].shape[1]
    n = w.shape[2]
    tn = min(tn, n)
    a_spec = pl.BlockSpec((tm, kq), lambda i, j: (i, 0))
    return pl.pallas_call(
        functools.partial(_mm_out_kernel, n_lat=n_lat),
        grid=(m // tm, n // tn),
        in_specs=[a_spec, a_spec, a_spec, a_spec,
                  pl.BlockSpec((1, 4 * kq, tn), lambda i, j: (layer, 0, j)),
                  pl.BlockSpec((tm, tn), lambda i, j: (i, j)),
                  pl.BlockSpec((2, tn), lambda i, j: (0, j))],
        out_specs=pl.BlockSpec((tm, tn), lambda i, j: (i, j)),
        out_shape=jax.ShapeDtypeStruct((m, n), F32),
        compiler_params=_cp("parallel", "parallel"),
        name="matmul_out",
    )(*parts, w, res, gate)


def _conv3_kernel(x_ref, w_ref, b_ref, o_ref, *, act, starts):
    x = x_ref[...]
    n = x.shape[0]
    row = lax.broadcasted_iota(jnp.int32, x.shape, 0)
    first = row == starts[0]
    last = row == n - 1
    for s in starts[1:]:
        first = first | (row == s)
        last = last | (row == s - 1)
    xm = jnp.where(first, 0.0, pltpu.roll(x, 1, 0))
    xp = jnp.where(last, 0.0, pltpu.roll(x, n - 1, 0))
    y = w_ref[0:1, :] * xm + w_ref[1:2, :] * x + w_ref[2:3, :] * xp + b_ref[...]
    if act:
        y = _silu(y)
    o_ref[...] = y


def conv3(x, w, b, act, starts=(0,), col0=0, tc=LANES):
    n, c = x.shape[0], w.shape[1]
    off = col0 // tc
    return pl.pallas_call(
        functools.partial(_conv3_kernel, act=act, starts=tuple(s for s in starts if s < n)),
        grid=(c // tc,),
        in_specs=[pl.BlockSpec((n, tc), lambda j: (0, j + off)),
                  pl.BlockSpec((3, tc), lambda j: (0, j)),
                  pl.BlockSpec((1, tc), lambda j: (0, j))],
        out_specs=pl.BlockSpec((n, tc), lambda j: (0, j)),
        out_shape=jax.ShapeDtypeStruct((n, c), F32),
        compiler_params=_cp("parallel"),
        name="conv3",
    )(x, w, b.reshape(1, c))


def _hyfilt_kernel(fv_ref, w1_ref, b1_ref, fr1_ref, w2_ref, b2_ref, fr2_ref, w3_ref, dl_ref,
                   k_ref, as_ref, *, seq, nfull, tj):
    i = pl.program_id(0)
    j = i * tj + lax.broadcasted_iota(jnp.int32, (tj, 1), 0)
    valid = (j < seq) | (j > nfull - seq)
    tf = jnp.where(j < seq, j, nfull - j).astype(F32)
    t_lin = tf / (seq - 1)
    wpos = (2.0 * math.pi) * tf / seq
    lane = lax.broadcasted_iota(jnp.int32, (tj, LANES), 1)
    arg = wpos * fv_ref[...]
    feats = jnp.where(lane == 0, t_lin,
                      jnp.where(lane <= HY_BANDS, jnp.cos(arg),
                                jnp.where(lane <= 2 * HY_BANDS, -jnp.sin(arg), 0.0)))
    h = jnp.sin(fr1_ref[...] * (jnp.dot(feats, w1_ref[...], precision=HI, preferred_element_type=F32) + b1_ref[...]))
    h = jnp.sin(fr2_ref[...] * (jnp.dot(h, w2_ref[...], precision=HI, preferred_element_type=F32) + b2_ref[...]))
    h = jnp.dot(h.astype(BF16), w3_ref[0].astype(BF16), preferred_element_type=F32)
    kv = jnp.where(valid, h * jnp.exp(-t_lin * dl_ref[...]), 0.0)
    k_ref[...] = kv.astype(k_ref.dtype)

    @pl.when(i == 0)
    def _():
        as_ref[...] = jnp.zeros_like(as_ref)

    as_ref[0:1, :] += jnp.sum(jnp.abs(kv), axis=0, keepdims=True)


def hyena_filter(seq, nfull, w1, b1, fr1, w2, b2, fr2, w3):
    tj = min(512, nfull)
    nblk = nfull // tj
    f = jnp.linspace(1e-4, HY_BANDS - 1, HY_BANDS, dtype=F32)
    fv = jnp.concatenate([jnp.zeros((1,), F32), f, f, jnp.zeros((LANES - 2 * HY_BANDS - 1,), F32)]).reshape(1, LANES)
    w1p = jnp.pad(w1.astype(F32), ((0, LANES - w1.shape[0]), (0, 0)))
    w3r = w3.astype(F32).reshape(HY_FFN, 2, 2, HY_C).transpose(2, 0, 1, 3).reshape(2, HY_FFN, 2 * HY_C)
    deltas = jnp.abs(jnp.linspace(math.log(1e-2) / 1.5, math.log(1e-2) / 0.3, HY_C, dtype=F32))
    dl = jnp.concatenate([deltas, deltas]).reshape(1, 2 * HY_C)
    small = lambda r, c: pl.BlockSpec((r, c), lambda i: (0, 0))
    return pl.pallas_call(
        functools.partial(_hyfilt_kernel, seq=seq, nfull=nfull, tj=tj),
        grid=(nblk,),
        in_specs=[small(1, LANES), small(LANES, HY_FFN), small(1, HY_FFN), small(1, HY_FFN),
                  small(HY_FFN, HY_FFN), small(1, HY_FFN), small(1, HY_FFN),
                  pl.BlockSpec((1, HY_FFN, 2 * HY_C), lambda i: ((2 * i) // nblk, 0, 0)),
                  small(1, 2 * HY_C)],
        out_specs=[pl.BlockSpec((tj, 2 * HY_C), lambda i: (i, 0)),
                   pl.BlockSpec((8, 2 * HY_C), lambda i: (0, 0))],
        out_shape=[jax.ShapeDtypeStruct((nfull, 2 * HY_C), BF16),
                   jax.ShapeDtypeStruct((8, 2 * HY_C), F32)],
        compiler_params=_cp("arbitrary"),
        name="hyena_filter",
    )(fv, w1p, b1.reshape(1, -1).astype(F32), fr1.reshape(1, -1).astype(F32), w2.astype(F32),
      b2.reshape(1, -1).astype(F32), fr2.reshape(1, -1).astype(F32), w3r, dl)


def _dft_tables(n1, rows_in, rows_out):
    nfull = n1 * LANES
    k1 = np.arange(n1)[:, None]
    a = np.arange(n1)[None, :]
    th = 2.0 * np.pi * k1 * a / n1
    frow = np.concatenate([np.cos(th), -np.sin(th)], axis=0)[:, :rows_in]
    b = np.arange(LANES)[None, :]
    tw = 2.0 * np.pi * k1 * b / nfull
    twr, twi = np.cos(tw), -np.sin(tw)
    ph = 2.0 * np.pi * np.arange(LANES)[:, None] * np.arange(LANES)[None, :] / LANES
    fr, fi = np.cos(ph), -np.sin(ph)
    f2 = np.block([[fr, fi], [-fi, fr]])
    f2i = np.block([[fr, -fi], [fi, fr]])
    thi = 2.0 * np.pi * np.arange(rows_out)[:, None] * np.arange(n1)[None, :] / n1
    finv = np.concatenate([np.cos(thi), -np.sin(thi)], axis=1) / nfull
    c = lambda m, dt: jnp.asarray(m, dtype=dt)
    return dict(frow=c(frow, BF16), twr=c(twr, F32), twi=c(twi, F32), f2=c(f2, BF16), f2i=c(f2i, BF16),
                finv=c(finv, BF16))


def _hy_forward(z_bf, frow_ref, twr_ref, twi_ref, pbuf, qbuf, *, cb, n1, scale=None):
    p = jnp.dot(frow_ref[...], z_bf, preferred_element_type=F32)
    pbuf[...] = p if scale is None else p * scale
    twr, twi = twr_ref[...], twi_ref[...]
    for c in range(cb):
        pr = pbuf[0:n1, c * LANES:(c + 1) * LANES]
        pi = pbuf[n1:2 * n1, c * LANES:(c + 1) * LANES]
        qbuf[c * n1:(c + 1) * n1, 0:LANES] = (pr * twr - pi * twi).astype(BF16)
        qbuf[c * n1:(c + 1) * n1, LANES:2 * LANES] = (pr * twi + pi * twr).astype(BF16)


def _hyspec_kernel(k_ref, sc_ref, frow_ref, twr_ref, twi_ref, f2_ref, o_ref, pbuf, qbuf, *, cb, n1):
    _hy_forward(k_ref[...], frow_ref, twr_ref, twi_ref, pbuf, qbuf, cb=cb, n1=n1, scale=sc_ref[...])
    o_ref[...] = jnp.dot(qbuf[...], f2_ref[...], preferred_element_type=F32).astype(o_ref.dtype)


def hyena_spectrum(kmat, scale, tabs, n1, cb):
    nch = kmat.shape[1] // LANES
    full = lambda s: pl.BlockSpec(s, lambda i: (0, 0))
    return pl.pallas_call(
        functools.partial(_hyspec_kernel, cb=cb, n1=n1),
        grid=(nch // cb,),
        in_specs=[pl.BlockSpec((n1, cb * LANES), lambda i: (0, i)),
                  pl.BlockSpec((1, cb * LANES), lambda i: (0, i)),
                  full((2 * n1, n1)), full((n1, LANES)), full((n1, LANES)), full((2 * LANES, 2 * LANES))],
        out_specs=pl.BlockSpec((cb * n1, 2 * LANES), lambda i: (i, 0)),
        out_shape=jax.ShapeDtypeStruct((nch * n1, 2 * LANES), BF16),
        scratch_shapes=[pltpu.VMEM((2 * n1, cb * LANES), F32), pltpu.VMEM((cb * n1, 2 * LANES), BF16)],
        compiler_params=_cp("parallel"),
        name="hyena_spectrum",
    )(kmat, scale, tabs["frow_full"], tabs["twr"], tabs["twi"], tabs["f2"])


def _hyconv_kernel(z_ref, g_ref, ks_ref, bias_ref, frow_ref, twr_ref, twi_ref, f2_ref, f2i_ref, finv_ref,
                   o_ref, pbuf, qbuf, sbuf, *, cb, n1, sub):
    z = z_ref[...]
    _hy_forward(z.astype(BF16), frow_ref, twr_ref, twi_ref, pbuf, qbuf, cb=cb, n1=n1)
    twr, twi = twr_ref[...], twi_ref[...]
    rows = sub * n1
    for s in range(cb // sub):
        rs = slice(s * rows, (s + 1) * rows)
        zz = jnp.dot(qbuf[rs, :], f2_ref[...], preferred_element_type=F32)
        zr, zi = zz[:, 0:LANES], zz[:, LANES:]
        kr, ki = ks_ref[rs, 0:LANES].astype(F32), ks_ref[rs, LANES:2 * LANES].astype(F32)
        y = jnp.concatenate([zr * kr - zi * ki, zr * ki + zi * kr], axis=1).astype(BF16)
        rr = jnp.dot(y, f2i_ref[...], preferred_element_type=F32)
        for cc in range(sub):
            c = s * sub + cc
            r_re = rr[cc * n1:(cc + 1) * n1, 0:LANES]
            r_im = rr[cc * n1:(cc + 1) * n1, LANES:]
            sbuf[0:n1, c * LANES:(c + 1) * LANES] = (r_re * twr + r_im * twi).astype(BF16)
            sbuf[n1:2 * n1, c * LANES:(c + 1) * LANES] = (r_im * twr - r_re * twi).astype(BF16)
    y = jnp.dot(finv_ref[...], sbuf[...], preferred_element_type=F32)
    o_ref[...] = (y + z * bias_ref[...]) * g_ref[...]


def hyena_conv(z, gate, kspec, order, bias, tabs, n1, cb, sub):
    rows = z[0].shape[0]
    nch, width = HY_C, HY_C * LANES
    koff = order * (nch // cb)
    full = lambda s: pl.BlockSpec(s, lambda i: (0, 0))
    io = pl.BlockSpec((rows, cb * LANES), lambda i: (0, i))
    sel = lambda a: pl.BlockSpec((rows, cb * LANES), lambda i: (0, i + a[1] // cb))
    return pl.pallas_call(
        functools.partial(_hyconv_kernel, cb=cb, n1=n1, sub=sub),
        grid=(nch // cb,),
        in_specs=[sel(z), sel(gate),
                  pl.BlockSpec((cb * n1, 2 * LANES), lambda i: (i + koff, 0)),
                  pl.BlockSpec((1, cb * LANES), lambda i: (0, i)),
                  full((2 * n1, rows)), full((n1, LANES)), full((n1, LANES)),
                  full((2 * LANES, 2 * LANES)), full((2 * LANES, 2 * LANES)), full((rows, 2 * n1))],
        out_specs=io,
        out_shape=jax.ShapeDtypeStruct((rows, width), F32),
        scratch_shapes=[pltpu.VMEM((2 * n1, cb * LANES), F32), pltpu.VMEM((cb * n1, 2 * LANES), BF16),
                        pltpu.VMEM((2 * n1, cb * LANES), BF16)],
        compiler_params=_cp("parallel"),
        name="hyena_conv",
    )(z[0], gate[0], kspec, bias, tabs["frow"], tabs["twr"], tabs["twi"], tabs["f2"], tabs["f2i"], tabs["finv"])


def _to_ab(x, rows):
    n, c = x.shape
    y = x.reshape(n // LANES, LANES, c).transpose(0, 2, 1).reshape(n // LANES, c * LANES)
    return jnp.pad(y, ((0, rows - n // LANES), (0, 0)))


def _from_ab(y, n):
    c = y.shape[1] // LANES
    return y[:n // LANES].reshape(n // LANES, c, LANES).transpose(0, 2, 1).reshape(n, c)


def hyena_mixer(xc, filt, bias):
    seq = xc.shape[0]
    if seq >= 1024:
        n1, rows, cb, sub = 2 * seq // LANES, seq // LANES, 16, 2
    else:
        n1, rows, cb, sub = 16, 16, 128, 16
    nfull = n1 * LANES
    tabs = _dft_tables(n1, rows, rows)
    tabs["frow_full"] = _dft_tables(n1, n1, rows)["frow"]
    kfull, asum = hyena_filter(seq, nfull, *filt)
    scale = jnp.repeat(1.0 / (asum[0] + EPS), LANES).reshape(1, -1)
    kspec = hyena_spectrum(_to_ab(kfull, n1), scale, tabs, n1, cb)
    xab = _to_ab(xc, rows)
    bias_l = jnp.repeat(bias.astype(F32), LANES, axis=1)
    z = hyena_conv((xab, 2 * HY_C), (xab, 0), kspec, 0, bias_l[0:1], tabs, n1, cb, sub)
    y = hyena_conv((z, 0), (xab, HY_C), kspec, 1, bias_l[1:2], tabs, n1, cb, sub)
    return _from_ab(y, seq)


def _dscan_kernel(*refs, rev, n_groups, upg, hpu, wd, use_kap):
    if use_kap:
        q_ref, k_ref, v_ref, la_ref, kap_ref, y_ref, st_ref = refs
    else:
        q_ref, k_ref, v_ref, la_ref, y_ref, st_ref = refs
    i = pl.program_id(0)

    @pl.when(i == 0)
    def _():
        st_ref[...] = jnp.zeros_like(st_ref)

    c = q_ref.shape[0]
    row = lax.broadcasted_iota(jnp.int32, (c, c), 0)
    col = lax.broadcasted_iota(jnp.int32, (c, c), 1)
    mask = (row <= col) if rev else (row >= col)
    tri = jnp.where(mask, 1.0, 0.0).astype(BF16)
    cum = _cumsum_rows(tri, la_ref[...])
    cum_t = cum.T
    c_end = cum[0:1, :] if rev else cum[c - 1:c, :]
    e_all = jnp.exp(cum)
    w_all = jnp.exp(c_end - cum)
    e_end = jnp.exp(c_end)
    if use_kap:
        kap = kap_ref[...]
        w_all = w_all * kap
    lane = lax.broadcasted_iota(jnp.int32, (c, wd), 1)
    lane1 = lax.broadcasted_iota(jnp.int32, (1, wd), 1)
    sub = wd // hpu

    def lanesel(cols, ln):
        out = cols[-1]
        for hh in range(hpu - 2, -1, -1):
            out = jnp.where(ln < (hh + 1) * sub, cols[hh], out)
        return out

    for g in range(n_groups):
        qg = q_ref[:, g * LANES:(g + 1) * LANES].astype(BF16)
        kg = k_ref[:, g * LANES:(g + 1) * LANES]
        gm = lax.dot_general(qg, kg.astype(BF16), (((1,), (1,)), ((), ())), preferred_element_type=F32)
        kg_t = kg.T.astype(BF16)
        for uu in range(upg):
            u = g * upg + uu
            v = v_ref[:, u * wd:(u + 1) * wd]
            heads = [u * hpu + hh for hh in range(hpu)]
            st = st_ref[u]
            y = lanesel([e_all[:, h:h + 1] for h in heads], lane) * jnp.dot(
                qg, st.astype(BF16), preferred_element_type=F32)
            for hh, h in enumerate(heads):
                dmat = jnp.exp(jnp.where(mask, cum[:, h:h + 1] - cum_t[h:h + 1, :], NEG))
                vh = v * kap[:, h:h + 1] if use_kap else v
                if hpu > 1:
                    vh = jnp.where((lane >= hh * sub) & (lane < (hh + 1) * sub), vh, 0.0)
                y = y + jnp.dot((gm * dmat).astype(BF16), vh.astype(BF16), preferred_element_type=F32)
            y_ref[:, u * wd:(u + 1) * wd] = y
            vw = v * lanesel([w_all[:, h:h + 1] for h in heads], lane)
            st_ref[u] = (lanesel([e_end[:, h:h + 1] for h in heads], lane1) * st
                         + jnp.dot(kg_t, vw.astype(BF16), preferred_element_type=F32))


def _chunk_order(n, n_lat, c, rev):
    nl, nt = n_lat // c, n // c
    if rev:
        return lambda i: nt - 1 - i
    return lambda i: jnp.where(i < nt - nl, nl + i, i - (nt - nl))


def decay_scan(q, k, v, la, kap, n_lat, *, rev, n_groups, upg, hpu, wd):
    n = q[0].shape[0]
    c = SCAN_CHUNK
    n_units = n_groups * upg
    rmap = _chunk_order(n, n_lat, c, rev)

    def spec(width, blk):
        return pl.BlockSpec((c, width), lambda i: (rmap(i), blk))

    ins = [q, k, v, la] + ([kap] if kap is not None else [])
    widths = [n_groups * LANES, n_groups * LANES, n_units * wd, LANES] + ([LANES] if kap is not None else [])
    return pl.pallas_call(
        functools.partial(_dscan_kernel, rev=rev, n_groups=n_groups, upg=upg, hpu=hpu, wd=wd,
                          use_kap=kap is not None),
        grid=(n // c,),
        in_specs=[spec(w, a[1]) for w, a in zip(widths, ins)],
        out_specs=pl.BlockSpec((c, n_units * wd), lambda i: (rmap(i), 0)),
        out_shape=jax.ShapeDtypeStruct((n, n_units * wd), F32),
        scratch_shapes=[pltpu.VMEM((n_units, LANES, wd), F32)],
        compiler_params=_cp("arbitrary"),
        name="decay_scan_rev" if rev else "decay_scan_fwd",
    )(*[a[0] for a in ins])


def _post_kernel(*refs, n_terms, skip, pre_gate, ngroups):
    terms = refs[:n_terms]
    pos = n_terms
    y = terms[0][...]
    for t in terms[1:]:
        y = y + t[...]
    if skip:
        y = y + refs[pos][...] * refs[pos + 1][...]
        pos += 2
    gate_ref, ng_ref, o_ref = refs[pos:pos + 3]
    gate = _silu(gate_ref[...])
    if pre_gate:
        y = y * gate
    gw = y.shape[1] // ngroups
    for g in range(ngroups):
        sl = slice(g * gw, (g + 1) * gw)
        yg = y[:, sl]
        o = yg * lax.rsqrt(jnp.mean(yg * yg, axis=-1, keepdims=True) + EPS) * ng_ref[:, sl]
        if not pre_gate:
            o = o * gate[:, sl]
        o_ref[:, sl] = o.astype(o_ref.dtype)


def mixer_post(terms, skip, gate, norm_g, *, pre_gate, ngroups, width=1024, tm=256):
    n = terms[0][0].shape[0]
    tm = min(tm, n)
    blk = lambda a: pl.BlockSpec((tm, width), lambda i: (i, a[1]))
    vec = pl.BlockSpec((1, width), lambda i: (0, 0))
    arrays = [t[0] for t in terms]
    specs = [blk(t) for t in terms]
    if skip is not None:
        arrays += [skip[0][0], skip[1].reshape(1, width)]
        specs += [blk(skip[0]), vec]
    arrays += [gate[0], norm_g.reshape(1, width)]
    specs += [blk(gate), vec]
    return pl.pallas_call(
        functools.partial(_post_kernel, n_terms=len(terms), skip=skip is not None, pre_gate=pre_gate,
                          ngroups=ngroups),
        grid=(n // tm,),
        in_specs=specs,
        out_specs=pl.BlockSpec((tm, width), lambda i: (i, 0)),
        out_shape=jax.ShapeDtypeStruct((n, width), BF16),
        compiler_params=_cp("parallel"),
        name="mixer_post",
    )(*arrays)


def _ssd_dt_kernel(x_ref, b_ref, a_ref, dt_ref, la_ref):
    x = x_ref[...] + b_ref[...]
    dt = jnp.maximum(x, 0.0) + jnp.log1p(jnp.exp(-jnp.abs(x)))
    dt_ref[...] = dt
    la_ref[...] = dt * a_ref[...]


def ssd_mixer(u, dt_raw, conv_w, conv_b, dt_bias, a_log, d_skip, norm_g, n_lat):
    n = u.shape[0]
    xbc = conv3(u, conv_w, conv_b, True, starts=(0, n_lat), col0=SSD_INNER)
    pad = lambda t: jnp.pad(t.astype(F32), ((0, 0), (0, LANES - SSD_HEADS))).reshape(1, 2 * LANES)
    tm = 256
    dt, la = pl.pallas_call(
        _ssd_dt_kernel,
        grid=(n // tm,),
        in_specs=[pl.BlockSpec((tm, 2 * LANES), lambda i: (i, 0)),
                  pl.BlockSpec((1, 2 * LANES), lambda i: (0, 0)),
                  pl.BlockSpec((1, 2 * LANES), lambda i: (0, 0))],
        out_specs=[pl.BlockSpec((tm, 2 * LANES), lambda i: (i, 0))] * 2,
        out_shape=[jax.ShapeDtypeStruct((n, 2 * LANES), F32)] * 2,
        compiler_params=_cp("parallel"),
        name="ssd_dt",
    )(dt_raw, pad(dt_bias), pad(-jnp.exp(a_log.astype(F32))))
    ys = [(decay_scan((xbc, 3), (xbc, 2), (xbc, 0), (la, d), (dt, d), n_lat,
                      rev=bool(d), n_groups=SSD_GROUPS, upg=2, hpu=2, wd=LANES), 0) for d in range(2)]
    dsk = jnp.repeat(d_skip.astype(F32), SSD_P)
    return mixer_post(ys, ((xbc, 0), dsk), (u, 0), norm_g, pre_gate=True, ngroups=SSD_GROUPS)


def _rope_kernel(q_ref, k_ref, inv_ref, qo_ref, ko_ref, *, n_lat, n_ctx):
    tm = q_ref.shape[0]
    row = pl.program_id(0) * tm + lax.broadcasted_iota(jnp.int32, (tm, 1), 0)
    pos = jnp.where(row < n_lat, row + n_ctx, row - n_lat).astype(F32)
    ang = pos * inv_ref[...]
    cs, sn = jnp.cos(ang), jnp.sin(ang)
    lane = lax.broadcasted_iota(jnp.int32, (tm, LANES), 1)
    sgn = jnp.where(lane < RET_DK // 2, -sn, sn)
    for h in range(RET_HEADS):
        sl = slice(h * RET_DK, (h + 1) * RET_DK)
        for src, dst, scale in ((q_ref, qo_ref, 1.0), (k_ref, ko_ref, RET_DK ** -0.5)):
            x = src[:, sl]
            y = x * cs + pltpu.roll(x, RET_DK // 2, 1) * sgn
            dst[:, sl] = y * scale


def retention_mixer(u, norm_g, n_lat):
    n = u.shape[0]
    tm = 256
    half = RET_DK // 2
    inv = ROPE_BASE ** (-jnp.arange(half, dtype=F32) / half)
    inv2 = jnp.concatenate([inv, inv]).reshape(1, LANES)
    wq = RET_HEADS * RET_DK
    qr, kr = pl.pallas_call(
        functools.partial(_rope_kernel, n_lat=n_lat, n_ctx=n - n_lat),
        grid=(n // tm,),
        in_specs=[pl.BlockSpec((tm, wq), lambda i: (i, 0)), pl.BlockSpec((tm, wq), lambda i: (i, 1)),
                  pl.BlockSpec((1, LANES), lambda i: (0, 0))],
        out_specs=[pl.BlockSpec((tm, wq), lambda i: (i, 0))] * 2,
        out_shape=[jax.ShapeDtypeStruct((n, wq), F32)] * 2,
        compiler_params=_cp("parallel"),
        name="rope",
    )(u, u, inv2)
    log_gamma = jnp.log1p(-jnp.exp2(-5.0 - jnp.arange(RET_HEADS, dtype=F32)))
    la = jnp.broadcast_to(jnp.pad(log_gamma, (0, LANES - RET_HEADS)).reshape(1, LANES), (n, LANES))
    ys = [(decay_scan((qr, 0), (kr, 0), (u, 1), (la, 0), None, n_lat,
                      rev=bool(d), n_groups=RET_HEADS, upg=1, hpu=1, wd=RET_DV), 0) for d in range(2)]
    return mixer_post(ys, None, (u, 2), norm_g, pre_gate=False, ngroups=RET_HEADS)


def _anchor_rows(gc, b, rev):
    c, n = gc.shape
    shift = b if rev else b - 1
    if 2 * b >= 8:
        return jnp.concatenate([jnp.broadcast_to(gc[s + shift:s + shift + 1, :], (2 * b, n))
                                for s in range(0, c, 2 * b)], axis=0)
    sub = lax.broadcasted_iota(jnp.int32, (8, n), 0)
    tiles = []
    for t in range(0, c, 8):
        tile = None
        for m in range(8 // (2 * b)):
            r = t + m * 2 * b + shift
            cand = jnp.broadcast_to(gc[r:r + 1, :], (8, n))
            tile = cand if tile is None else jnp.where(sub >= m * 2 * b, cand, tile)
        tiles.append(tile)
    return jnp.concatenate(tiles, axis=0)


def _hg_kernel(q_ref, f_ref, v_ref, lb_ref, y_ref, st_ref, *, rev):
    i = pl.program_id(0)

    @pl.when(i == 0)
    def _():
        st_ref[...] = jnp.zeros_like(st_ref)

    c, width = q_ref.shape
    nh = width // HG_DK
    lb = lb_ref[...]
    qh = _silu(q_ref[...])
    fr = f_ref[...]
    lg = jnp.log(lb + (1.0 - lb) * jax.nn.sigmoid(fr))
    kk = (1.0 - lb) * jax.nn.sigmoid(-fr)
    row = lax.broadcasted_iota(jnp.int32, (c, c), 0)
    col = lax.broadcasted_iota(jnp.int32, (c, c), 1)
    tri = jnp.where((row <= col) if rev else (row >= col), 1.0, 0.0).astype(BF16)
    gc = _cumsum_rows(tri, lg)
    g_end = gc[0:1, :] if rev else gc[c - 1:c, :]
    qe = (qh * jnp.exp(gc)).astype(BF16)
    kw = (kk * jnp.exp(g_end - gc)).astype(BF16)
    dec = jnp.exp(g_end)
    vb = v_ref[...].astype(BF16)
    qb, kb = qh.astype(BF16), kk.astype(BF16)
    nt = (((1,), (1,)), ((), ()))
    hs = [slice(h * HG_DK, (h + 1) * HG_DK) for h in range(nh)]
    scores = [jnp.where(row == col, lax.dot_general(qb[:, s], kb[:, s], nt, preferred_element_type=F32), 0.0)
              for s in hs]
    ridx = lax.broadcasted_iota(jnp.int32, (c, 1), 0)
    lvl = 0
    while (1 << lvl) < c:
        b = 1 << lvl
        anchor = _anchor_rows(gc, b, rev)
        upper = ((ridx >> lvl) & 1) == 1
        q_on, k_on = (~upper, upper) if rev else (upper, ~upper)
        qt = (qh * jnp.exp(jnp.where(q_on, gc - anchor, NEG))).astype(BF16)
        kt = (kk * jnp.exp(jnp.where(k_on, anchor - gc, NEG))).astype(BF16)
        same = (row >> (lvl + 1)) == (col >> (lvl + 1))
        for h, s in enumerate(hs):
            sc = lax.dot_general(qt[:, s], kt[:, s], nt, preferred_element_type=F32)
            scores[h] = scores[h] + jnp.where(same, sc, 0.0)
        lvl += 1
    for h, s in enumerate(hs):
        st = st_ref[h]
        y = jnp.dot(scores[h].astype(BF16), vb[:, s], preferred_element_type=F32)
        y = y + lax.dot_general(qe[:, s], st.astype(BF16), nt, preferred_element_type=F32)
        y_ref[:, s] = y
        st_ref[h] = dec[:, s] * st + lax.dot_general(vb[:, s], kw[:, s], (((0,), (0,)), ((), ())),
                                                     preferred_element_type=F32)


def hgrn2_scan(u, lb, n_lat, *, rev, d):
    n = u.shape[0]
    c = HG_CHUNK
    rmap = _chunk_order(n, n_lat, c, rev)
    spec = lambda blk: pl.BlockSpec((c, HG_W), lambda i: (rmap(i), blk))
    return pl.pallas_call(
        functools.partial(_hg_kernel, rev=rev),
        grid=(n // c,),
        in_specs=[spec(0), spec(1 + d), spec(3), pl.BlockSpec((1, HG_W), lambda i: (0, 0))],
        out_specs=pl.BlockSpec((c, HG_W), lambda i: (rmap(i), 0)),
        out_shape=jax.ShapeDtypeStruct((n, HG_W), F32),
        scratch_shapes=[pltpu.VMEM((HG_HEADS, HG_DK, HG_DK), F32)],
        compiler_params=_cp("arbitrary"),
        name="hgrn2_scan_rev" if rev else "hgrn2_scan_fwd",
    )(u, u, u, lb)


def hgrn2_mixer(u, lb, norm_g, n_lat):
    ys = [(hgrn2_scan(u, lb[d:d + 1].astype(F32), n_lat, rev=bool(d), d=d), 0) for d in range(2)]
    return mixer_post(ys, None, (u, 4), norm_g, pre_gate=False, ngroups=HG_HEADS)


_HIGH16 = 0xFFFF0000


def _pack_rows(y, o_ref):
    t, d = y.shape
    nj = d // (2 * LANES)
    for j in range(nj):
        lo = y[:, 2 * j * LANES:(2 * j + 1) * LANES].astype(BF16).astype(F32)
        hi = y[:, (2 * j + 1) * LANES:(2 * j + 2) * LANES].astype(BF16).astype(F32)
        word = (pltpu.bitcast(lo, jnp.uint32) >> 16) | (pltpu.bitcast(hi, jnp.uint32) & jnp.uint32(_HIGH16))
        o_ref[pl.ds(j, t, stride=nj), :] = word


def _unpack_piece(p_ref, j, t, nj):
    word = p_ref[pl.ds(j, t, stride=nj), :]
    lo = pltpu.bitcast(word << 16, F32)
    hi = pltpu.bitcast(word & jnp.uint32(_HIGH16), F32)
    return lo, hi


def _router_kernel(x_ref, g_ref, sc_ref, sh_ref, wr_ref, br_ref, h_ref, r_ref, *, n_lat):
    x = x_ref[...]
    tm = x.shape[0]
    y = x * lax.rsqrt(jnp.mean(x * x, axis=-1, keepdims=True) + EPS) * g_ref[...]
    h = y * (1.0 + _row_select(sc_ref, tm, n_lat)) + _row_select(sh_ref, tm, n_lat)
    _pack_rows(h, h_ref)
    lg = jnp.dot(h, wr_ref[...], precision=HI, preferred_element_type=F32) + br_ref[...]
    lane = lax.broadcasted_iota(jnp.int32, lg.shape, 1).astype(F32)
    big = float(LANES)
    is_g = lane < MOE_GROUPS
    gmax = jnp.max(jnp.where(is_g, lg, -jnp.inf), axis=1, keepdims=True)
    gsel = jnp.min(jnp.where(is_g & (lg == gmax), lane, big), axis=1, keepdims=True)
    pg = 1.0 / jnp.sum(jnp.where(is_g, jnp.exp(lg - gmax), 0.0), axis=1, keepdims=True)
    lo = MOE_GROUPS + MOE_PER_GROUP * gsel
    in_e = (lane >= lo) & (lane < lo + MOE_PER_GROUP)
    v1 = jnp.max(jnp.where(in_e, lg, -jnp.inf), axis=1, keepdims=True)
    i1 = jnp.min(jnp.where(in_e & (lg == v1), lane, big), axis=1, keepdims=True)
    rest = in_e & (lane != i1)
    v2 = jnp.max(jnp.where(rest, lg, -jnp.inf), axis=1, keepdims=True)
    i2 = jnp.min(jnp.where(rest & (lg == v2), lane, big), axis=1, keepdims=True)
    t = jnp.exp(v2 - v1)
    w1 = pg / (1.0 + t)
    w2 = pg * t / (1.0 + t)
    r_ref[...] = jnp.where(lane == 0, i1 - MOE_GROUPS,
                           jnp.where(lane == 1, i2 - MOE_GROUPS,
                                     jnp.where(lane == 2, w1, jnp.where(lane == 3, w2, 0.0))))


def moe_router(x, g, sc, sh, wr, br, m, n_lat, tm=256):
    d = x.shape[1]
    nj = d // (2 * LANES)
    vec = pl.BlockSpec((1, d), lambda i: (0, 0))
    two = pl.BlockSpec((2, d), lambda i: (0, 0))
    return pl.pallas_call(
        functools.partial(_router_kernel, n_lat=n_lat),
        grid=(m // tm,),
        in_specs=[pl.BlockSpec((tm, d), lambda i: (i, 0)), vec, two, two,
                  pl.BlockSpec((d, LANES), lambda i: (0, 0)), pl.BlockSpec((1, LANES), lambda i: (0, 0))],
        out_specs=[pl.BlockSpec((tm * nj, LANES), lambda i: (i, 0)), pl.BlockSpec((tm, LANES), lambda i: (i, 0))],
        out_shape=[jax.ShapeDtypeStruct((m * nj, LANES), jnp.uint32), jax.ShapeDtypeStruct((m, LANES), F32)],
        compiler_params=_cp("parallel"),
        name="moe_router",
    )(x, g.reshape(1, d), sc, sh, wr, br)


def _moe_plan(route, m, n_tiles):
    t = MOE_TILE
    pe = jnp.concatenate([route[:, 0], route[:, 1]]).astype(jnp.int32)
    onehot = (pe[:, None] == jnp.arange(MOE_EXPERTS, dtype=jnp.int32)[None, :]).astype(jnp.int32)
    csum = jnp.cumsum(onehot, axis=0)
    rank = jnp.take_along_axis(csum, pe[:, None], axis=1)[:, 0] - 1
    cnt = csum[-1]
    ntile = (cnt + t - 1) // t
    tend = jnp.cumsum(ntile)
    tstart = tend - ntile
    slot = tstart[pe] * t + rank
    tid = jnp.arange(n_tiles, dtype=jnp.int32)
    tile_e = jnp.clip(jnp.searchsorted(tend, tid, side="right"), 0, MOE_EXPERTS - 1).astype(jnp.int32)
    tile_valid = jnp.where(tid < tend[-1], jnp.clip(cnt[tile_e] - (tid - tstart[tile_e]) * t, 0, t), 0)
    tok = jnp.tile(jnp.arange(m, dtype=jnp.int32), 2)
    row_tok = jnp.zeros((n_tiles * t,), jnp.int32).at[slot].set(tok)
    return row_tok, tile_e, tile_valid.astype(jnp.int32), slot[:m], slot[m:]


def _row_copies(idx_ref, base, src_hbm, dst, sem, rows, nj, wait):
    def body(r, carry):
        src = 0 if wait else idx_ref[base + r] * nj
        cp = pltpu.make_async_copy(src_hbm.at[pl.ds(src, nj), :], dst.at[pl.ds(r * nj, nj), :], sem)
        if wait:
            cp.wait()
        else:
            cp.start()
        return carry
    lax.fori_loop(0, rows, body, 0)


def _expert_kernel(te_ref, tv_ref, tok_ref, hp_hbm, w1_ref, w3_ref, w2_ref, o_ref, buf, xbuf, sem, *, nj):
    i = pl.program_id(0)
    n = pl.num_programs(0)
    t = xbuf.shape[0]

    def fetch(tile, slot):
        _row_copies(tok_ref, tile * t, hp_hbm, buf.at[slot], sem.at[slot], t, nj, wait=False)

    @pl.when((i == 0) & (tv_ref[0] > 0))
    def _():
        fetch(0, 0)

    @pl.when((i + 1 < n) & (tv_ref[jnp.minimum(i + 1, n - 1)] > 0))
    def _():
        fetch(i + 1, (i + 1) % 2)

    @pl.when(tv_ref[i] > 0)
    def _():
        slot = i % 2
        cur = buf.at[slot]
        _row_copies(tok_ref, 0, hp_hbm, cur, sem.at[slot], t, nj, wait=True)
        for j in range(nj):
            lo, hi = _unpack_piece(cur, j, t, nj)
            xbuf[:, 2 * j * LANES:(2 * j + 1) * LANES] = lo.astype(BF16)
            xbuf[:, (2 * j + 1) * LANES:(2 * j + 2) * LANES] = hi.astype(BF16)
        x = xbuf[...]
        a = jnp.dot(x, w1_ref[0, 0], preferred_element_type=F32)
        b = jnp.dot(x, w3_ref[0, 0], preferred_element_type=F32)
        y = jnp.dot((_silu(a) * b).astype(BF16), w2_ref[0, 0], preferred_element_type=F32)
        _pack_rows(y, o_ref)

    @pl.when(tv_ref[i] == 0)
    def _():
        o_ref[...] = jnp.zeros_like(o_ref)


def moe_experts(hp, row_tok, tile_e, tile_valid, w1, w3, w2, layer):
    t = MOE_TILE
    n_tiles = tile_e.shape[0]
    d, ff = w1.shape[2], w1.shape[3]
    nj = d // (2 * LANES)
    wmap = lambda i, te, tv, tok: (layer, te[i], 0, 0)
    return pl.pallas_call(
        functools.partial(_expert_kernel, nj=nj),
        grid_spec=pltpu.PrefetchScalarGridSpec(
            num_scalar_prefetch=3, grid=(n_tiles,),
            in_specs=[pl.BlockSpec(memory_space=pl.ANY),
                      pl.BlockSpec((1, 1, d, ff), wmap), pl.BlockSpec((1, 1, d, ff), wmap),
                      pl.BlockSpec((1, 1, ff, d), wmap)],
            out_specs=pl.BlockSpec((t * nj, LANES), lambda i, te, tv, tok: (i, 0)),
            scratch_shapes=[pltpu.VMEM((2, t * nj, LANES), jnp.uint32), pltpu.VMEM((t, d), BF16),
                            pltpu.SemaphoreType.DMA((2,))]),
        out_shape=jax.ShapeDtypeStruct((n_tiles * t * nj, LANES), jnp.uint32),
        compiler_params=_cp("arbitrary"),
        name="moe_experts",
    )(tile_e, tile_valid, row_tok, hp, w1, w3, w2)


def _combine_kernel(s0_ref, s1_ref, x_ref, r_ref, g_ref, fg_ref, ys_hbm, o_ref, buf, sem, *, n_lat, final, nj):
    tt = x_ref.shape[0]
    base = pl.program_id(0) * tt
    for k, s_ref in enumerate((s0_ref, s1_ref)):
        _row_copies(s_ref, base, ys_hbm, buf.at[k], sem.at[k], tt, nj, wait=False)
    for k, s_ref in enumerate((s0_ref, s1_ref)):
        _row_copies(s_ref, base, ys_hbm, buf.at[k], sem.at[k], tt, nj, wait=True)
    r = r_ref[...]
    w0, w1 = r[:, 2:3], r[:, 3:4]
    for j in range(nj):
        p0 = _unpack_piece(buf.at[0], j, tt, nj)
        p1 = _unpack_piece(buf.at[1], j, tt, nj)
        for half in range(2):
            cs = slice((2 * j + half) * LANES, (2 * j + half + 1) * LANES)
            o_ref[:, cs] = x_ref[:, cs] + _row_select(g_ref.at[:, cs], tt, n_lat) * (w0 * p0[half] + w1 * p1[half])
    if final:
        y = o_ref[...]
        o_ref[...] = y * lax.rsqrt(jnp.mean(y * y, axis=-1, keepdims=True) + EPS) * fg_ref[...]


def moe_combine(x, route, gate, final_g, ys, slot0, slot1, m, n_lat, final, tt=256):
    d = x.shape[1]
    nj = d // (2 * LANES)
    return pl.pallas_call(
        functools.partial(_combine_kernel, n_lat=n_lat, final=final, nj=nj),
        grid_spec=pltpu.PrefetchScalarGridSpec(
            num_scalar_prefetch=2, grid=(m // tt,),
            in_specs=[pl.BlockSpec((tt, d), lambda i, a, b: (i, 0)),
                      pl.BlockSpec((tt, LANES), lambda i, a, b: (i, 0)),
                      pl.BlockSpec((2, d), lambda i, a, b: (0, 0)),
                      pl.BlockSpec((1, d), lambda i, a, b: (0, 0)),
                      pl.BlockSpec(memory_space=pl.ANY)],
            out_specs=pl.BlockSpec((tt, d), lambda i, a, b: (i, 0)),
            scratch_shapes=[pltpu.VMEM((2, tt * nj, LANES), jnp.uint32), pltpu.SemaphoreType.DMA((2,))]),
        out_shape=jax.ShapeDtypeStruct((m, d), F32),
        compiler_params=_cp("arbitrary"),
        name="moe_combine",
    )(slot0, slot1, x, route, gate, final_g.reshape(1, d), ys)


def hier_moe(x, norm_g, sc, sh, gate, P, final_g, m, n_lat, final):
    n_tiles = 2 * m // MOE_TILE + MOE_EXPERTS
    hp, route = moe_router(x, norm_g, sc, sh, P["moe_wr"], P["moe_br"], m, n_lat)
    row_tok, tile_e, tile_valid, slot0, slot1 = _moe_plan(route, m, n_tiles)
    ys = moe_experts(hp, row_tok, tile_e, tile_valid, P["moe_w1"], P["moe_w3"], P["moe_w2"], P["layer"])
    return moe_combine(x, route, gate, final_g, ys, slot0, slot1, m, n_lat, final)


def _to_cols(t, rows):
    n, ch = t.shape
    return jnp.swapaxes(t.reshape(rows, GRID_W, ch), 0, 1).reshape(n, ch)


def _from_cols(t, rows):
    n, ch = t.shape
    return jnp.swapaxes(t.reshape(GRID_W, rows, ch), 0, 1).reshape(n, ch)


def _layer(xa, mod, P, lb, n_lat, last, final_g):
    n, d = xa.shape
    sh1, sc1, g1, sh2, sc2, g2 = (mod[:, q * d:(q + 1) * d] for q in range(6))
    grid_rows = n_lat // GRID_W
    h = norm_modulate(xa, P["norm1"], sc1, sh1, n_lat)
    h_cols = jnp.concatenate([_to_cols(h[:n_lat], grid_rows), h[n_lat:]], axis=0)
    tm = 768 if n % 768 == 0 else 256
    layer, w_in = P["layer"], P["w_in"]
    u_hy = matmul(h, w_in, layer, W_IN_HY, 3072, tm)
    u_ssd = matmul(h, w_in, layer, W_IN_SSD, 3072, tm)
    u_hg = matmul(h_cols, w_in, layer, W_IN_HG, 5120, tm)
    u_ret = matmul(h, w_in, layer, W_IN_RET, 3072, tm)
    dt_raw = matmul(h, w_in, layer, W_IN_DT, 2 * LANES, tm, tn=2 * LANES)
    filt = (P["hy_w1"], P["hy_b1"], P["hy_fr1"], P["hy_w2"], P["hy_b2"], P["hy_fr2"], P["hy_w3"])
    xc = conv3(u_hy, P["hy_conv_w"], P["hy_conv_b"], False, starts=(0, n_lat))
    a = hyena_mixer(xc[:n_lat], filt, P["hy_bias"]).astype(BF16)
    if not last:
        a = jnp.concatenate([a, hyena_mixer(xc[n_lat:], filt, P["hy_bias"]).astype(BF16)], axis=0)
    b = ssd_mixer(u_ssd, dt_raw, P["ssd_conv_w"], P["ssd_conv_b"], P["ssd_dt_bias"], P["ssd_a_log"],
                  P["ssd_d"], P["ssd_norm_g"], n_lat)
    c = hgrn2_mixer(u_hg, lb, P["hg_norm_g"], n_lat)
    c = jnp.concatenate([_from_cols(c[:n_lat], grid_rows), c[n_lat:]], axis=0)
    dd = retention_mixer(u_ret, P["ret_norm_g"], n_lat)
    m = n_lat if last else n
    xa = matmul_out((a, b, c, dd), P["w_out"], layer, xa, g1, m, n_lat, tm=1024 if m % 1024 == 0 else tm)
    return hier_moe(xa, P["norm2"], sc2, sh2, g2, P, final_g, m, n_lat, last)


def kernel(x, c, ctx, c_ctx, ada_w, ada_b, norm1_g, norm2_g, w_in, w_out, hy_conv_w, hy_conv_b, hy_w1, hy_b1, hy_fr1, hy_w2, hy_b2, hy_fr2, hy_w3, hy_bias, ssd_conv_w, ssd_conv_b, ssd_dt_bias, ssd_a_log, ssd_d, ssd_norm_g, hg_lb_raw, hg_norm_g, ret_norm_g, moe_wg, moe_bg, moe_we, moe_be, moe_w1, moe_w3, moe_w2, final_g):
    depth = ada_w.shape[0]
    n_lat = x.shape[1]
    lb_prob = jax.nn.softmax(hg_lb_raw.astype(F32), axis=0)
    lb_all = jnp.cumsum(lb_prob, axis=0) - lb_prob[0]
    mods = ada_modulation(c, c_ctx, ada_w, ada_b)
    xa = jnp.concatenate([x[0], ctx[0]], axis=0)
    o_dt, o_hg = 6144, 6176
    zpad = jnp.zeros(w_in.shape[:2] + (LANES - SSD_HEADS,), w_in.dtype)
    w_in_b = jnp.concatenate([w_in[:, :, :o_dt], w_in[:, :, o_hg:],
                              w_in[:, :, o_dt:o_dt + SSD_HEADS], zpad,
                              w_in[:, :, o_dt + SSD_HEADS:o_hg], zpad], axis=2).astype(BF16)
    w_out_b = w_out.astype(BF16)
    moe_b = [w.astype(BF16) for w in (moe_w1, moe_w3, moe_w2)]
    for l in range(depth):
        pad_r = LANES - MOE_GROUPS - MOE_EXPERTS
        P = {
            "layer": l, "norm1": norm1_g[l], "norm2": norm2_g[l], "w_in": w_in_b, "w_out": w_out_b,
            "hy_conv_w": hy_conv_w[l], "hy_conv_b": hy_conv_b[l], "hy_w1": hy_w1[l], "hy_b1": hy_b1[l],
            "hy_fr1": hy_fr1[l], "hy_w2": hy_w2[l], "hy_b2": hy_b2[l], "hy_fr2": hy_fr2[l], "hy_w3": hy_w3[l],
            "hy_bias": hy_bias[l],
            "ssd_conv_w": ssd_conv_w[l], "ssd_conv_b": ssd_conv_b[l], "ssd_dt_bias": ssd_dt_bias[l],
            "ssd_a_log": ssd_a_log[l], "ssd_d": ssd_d[l], "ssd_norm_g": ssd_norm_g[l],
            "hg_norm_g": hg_norm_g[l], "ret_norm_g": ret_norm_g[l],
            "moe_wr": jnp.pad(jnp.concatenate([moe_wg[l], moe_we[l]], axis=1).astype(F32), ((0, 0), (0, pad_r))),
            "moe_br": jnp.pad(jnp.concatenate([moe_bg[l], moe_be[l]]).astype(F32), (0, pad_r)).reshape(1, LANES),
            "moe_w1": moe_b[0], "moe_w3": moe_b[1], "moe_w2": moe_b[2],
        }
        xa = _layer(xa, mods[l, 0:2], P, lb_all[l], n_lat, l == depth - 1, final_g)
    return xa[None]
```

```python
import functools
import math

import numpy as np
import jax
import jax.numpy as jnp
from jax import lax
from jax.experimental import pallas as pl
from jax.experimental.pallas import tpu as pltpu

F32 = jnp.float32
BF16 = jnp.bfloat16
EPS = 1e-6
NEG = -1e30
LANES = 128
VMEM_LIMIT = 56 * 1024 * 1024

D_MODEL = 4096
GRID_W = 64
HY_C = 1024
HY_BANDS = 16
HY_FFN = 64
SSD_HEADS = 16
SSD_P = 64
SSD_INNER = SSD_HEADS * SSD_P
SSD_GROUPS = 4
SSD_N = 128
HG_HEADS = 8
HG_DK = 128
HG_W = HG_HEADS * HG_DK
RET_HEADS = 4
RET_DK = 128
RET_DV = 256
RET_W = RET_HEADS * RET_DV
ROPE_BASE = 10000.0
MOE_GROUPS = 4
MOE_PER_GROUP = 8
MOE_EXPERTS = 32
MOE_FF = 512
MOE_TILE = 256
SCAN_CHUNK = 128
HG_CHUNK = 128
HI = lax.Precision.HIGHEST
W_IN_HY, W_IN_SSD, W_IN_HG, W_IN_RET = 0, 3072, 6144, 11264


def _cp(*sem):
    return pltpu.CompilerParams(dimension_semantics=sem, vmem_limit_bytes=VMEM_LIMIT)


def _silu(x):
    return x * jax.nn.sigmoid(x)


def _split3(x):
    hi = x.astype(BF16)
    r1 = x - hi.astype(F32)
    mid = r1.astype(BF16)
    lo = (r1 - mid.astype(F32)).astype(BF16)
    return hi, mid, lo


def _cumsum_rows(tri, x):
    hi, mid, lo = _split3(x)
    return (jnp.dot(tri, hi, preferred_element_type=F32)
            + jnp.dot(tri, mid, preferred_element_type=F32)
            + jnp.dot(tri, lo, preferred_element_type=F32))


def _ada_kernel(cb_ref, w_ref, b_ref, o_ref, acc_ref, *, nk):
    k = pl.program_id(2)

    @pl.when(k == 0)
    def _():
        acc_ref[...] = jnp.zeros_like(acc_ref)

    tk, tn = w_ref.shape[1], w_ref.shape[2]
    for s in range(2):
        sv = _silu(cb_ref[s])
        for j in range(tn // LANES):
            p = w_ref[0, :, j * LANES:(j + 1) * LANES] * sv
            acc_ref[s, :, j * LANES:(j + 1) * LANES] += p.reshape(tk // 8, 8, LANES).sum(axis=0)

    @pl.when(k == nk - 1)
    def _():
        r0 = acc_ref[0].sum(axis=0, keepdims=True) + b_ref[0]
        r1 = acc_ref[1].sum(axis=0, keepdims=True) + b_ref[0]
        row = lax.broadcasted_iota(jnp.int32, (8, tn), 0)
        o_ref[0] = jnp.where(row == 0, r0, jnp.where(row == 1, r1, 0.0))


def ada_modulation(c, c_ctx, ada_w, ada_b):
    depth, d, n = ada_w.shape
    tk, tn = min(512, d), math.gcd(n, 2048)
    cb = jnp.stack([jnp.broadcast_to(c.reshape(d, 1), (d, LANES)),
                    jnp.broadcast_to(c_ctx.reshape(d, 1), (d, LANES))])
    nk = d // tk
    return pl.pallas_call(
        functools.partial(_ada_kernel, nk=nk),
        grid=(depth, n // tn, nk),
        in_specs=[pl.BlockSpec((2, tk, LANES), lambda l, j, k: (0, k, 0)),
                  pl.BlockSpec((1, tk, tn), lambda l, j, k: (l, k, j)),
                  pl.BlockSpec((1, 1, tn), lambda l, j, k: (l, 0, j))],
        out_specs=pl.BlockSpec((1, 8, tn), lambda l, j, k: (l, 0, j)),
        out_shape=jax.ShapeDtypeStruct((depth, 8, n), F32),
        scratch_shapes=[pltpu.VMEM((2, 8, tn), F32)],
        compiler_params=_cp("arbitrary", "arbitrary", "arbitrary"),
        name="ada_modulation",
    )(cb, ada_w, ada_b.reshape(depth, 1, n))


def _row_select(ref, tm, n_lat):
    row = pl.program_id(0) * tm + lax.broadcasted_iota(jnp.int32, (tm, 1), 0)
    return jnp.where(row < n_lat, ref[0:1, :], ref[1:2, :])


def _norm_kernel(x_ref, g_ref, sc_ref, sh_ref, o_ref, lat_ref, *, n_lat):
    x = x_ref[...]
    tm = x.shape[0]
    y = x * lax.rsqrt(jnp.mean(x * x, axis=-1, keepdims=True) + EPS) * g_ref[...]
    h = (y * (1.0 + _row_select(sc_ref, tm, n_lat)) + _row_select(sh_ref, tm, n_lat)).astype(o_ref.dtype)
    o_ref[...] = h

    @pl.when(pl.program_id(0) * tm < n_lat)
    def _():
        lat_ref[...] = h


def norm_modulate(x, g, sc, sh, n_lat, out_dtype=BF16, tm=256):
    m, d = x.shape
    vec = pl.BlockSpec((1, d), lambda i: (0, 0))
    two = pl.BlockSpec((2, d), lambda i: (0, 0))
    last_lat = n_lat // tm - 1
    return pl.pallas_call(
        functools.partial(_norm_kernel, n_lat=n_lat),
        grid=(m // tm,),
        in_specs=[pl.BlockSpec((tm, d), lambda i: (i, 0)), vec, two, two],
        out_specs=[pl.BlockSpec((tm, d), lambda i: (i, 0)),
                   pl.BlockSpec((tm, d), lambda i: (jnp.minimum(i, last_lat), 0))],
        out_shape=[jax.ShapeDtypeStruct((m, d), out_dtype), jax.ShapeDtypeStruct((n_lat, d), out_dtype)],
        compiler_params=_cp("arbitrary"),
        name="norm_modulate",
    )(x, g.reshape(1, d), sc, sh)


def _regroup_kernel(a_ref, b_ref, o_ref, *, shift, first_shifted):
    tn = o_ref.shape[2]
    j = pl.program_id(2)

    @pl.when(j < first_shifted)
    def _():
        o_ref[0] = a_ref[0].astype(o_ref.dtype)

    @pl.when(j >= first_shifted)
    def _():
        x = jnp.concatenate([a_ref[0], b_ref[0]], axis=1)
        o_ref[0] = x[:, shift:shift + tn].astype(o_ref.dtype)


def regroup_w_in(w_in, cut, shift, tk=512, tn=512):
    depth, k, n = w_in.shape
    n_out = (n - shift) // tn * tn
    last = (n - 1) // tn
    return pl.pallas_call(
        functools.partial(_regroup_kernel, shift=shift, first_shifted=cut // tn),
        grid=(depth, k // tk, n_out // tn),
        in_specs=[pl.BlockSpec((1, tk, tn), lambda l, i, j: (l, i, j)),
                  pl.BlockSpec((1, tk, tn), lambda l, i, j: (l, i, jnp.minimum(j + 1, last)))],
        out_specs=pl.BlockSpec((1, tk, tn), lambda l, i, j: (l, i, j)),
        out_shape=jax.ShapeDtypeStruct((depth, k, n_out), BF16),
        compiler_params=_cp("parallel", "parallel", "parallel"),
        name="regroup_w_in",
    )(w_in, w_in)


def _mm_kernel(a_ref, w_ref, o_ref):
    o_ref[...] = jnp.dot(a_ref[...], w_ref[0], preferred_element_type=F32).astype(o_ref.dtype)


def matmul(a, w, layer, col0, n, tm, tn=512, out_dtype=F32):
    m, k = a.shape
    tn = min(tn, n)
    off = col0 // tn
    return pl.pallas_call(
        _mm_kernel,
        grid=(m // tm, n // tn),
        in_specs=[pl.BlockSpec((tm, k), lambda i, j: (i, 0)),
                  pl.BlockSpec((1, k, tn), lambda i, j: (layer, 0, j + off))],
        out_specs=pl.BlockSpec((tm, tn), lambda i, j: (i, j)),
        out_shape=jax.ShapeDtypeStruct((m, n), out_dtype),
        compiler_params=_cp("parallel", "parallel"),
        name="matmul",
    )(a, w)


def _mm_out_kernel(a0_ref, a1_ref, a2_ref, a3_ref, w_ref, r_ref, g_ref, o_ref, *, n_lat):
    tm, kq = a0_ref.shape
    acc = jnp.dot(a0_ref[...], w_ref[0, 0:kq, :], preferred_element_type=F32)
    for q, a_ref in enumerate((a1_ref, a2_ref, a3_ref), start=1):
        acc += jnp.dot(a_ref[...], w_ref[0, q * kq:(q + 1) * kq, :], preferred_element_type=F32)
    o_ref[...] = r_ref[...] + _row_select(g_ref, tm, n_lat) * acc


def matmul_out(parts, w, layer, res, gate, m, n_lat, tm, tn=512):
    kq = parts[0].shape[1]
    n = w.shape[2]
    tn = min(tn, n)
    a_spec = pl.BlockSpec((tm, kq), lambda i, j: (i, 0))
    return pl.pallas_call(
        functools.partial(_mm_out_kernel, n_lat=n_lat),
        grid=(m // tm, n // tn),
        in_specs=[a_spec, a_spec, a_spec, a_spec,
                  pl.BlockSpec((1, 4 * kq, tn), lambda i, j: (layer, 0, j)),
                  pl.BlockSpec((tm, tn), lambda i, j: (i, j)),
                  pl.BlockSpec((2, tn), lambda i, j: (0, j))],
        out_specs=pl.BlockSpec((tm, tn), lambda i, j: (i, j)),
        out_shape=jax.ShapeDtypeStruct((m, n), F32),
        compiler_params=_cp("parallel", "parallel"),
        name="matmul_out",
    )(*parts, w, res, gate)


def _conv3_kernel(x_ref, w_ref, b_ref, *o_refs, act, starts):
    x = x_ref[...]
    n = x.shape[0]
    row = lax.broadcasted_iota(jnp.int32, x.shape, 0)
    first = row == starts[0]
    last = row == n - 1
    for s in starts[1:]:
        first = first | (row == s)
        last = last | (row == s - 1)
    xm = jnp.where(first, 0.0, pltpu.roll(x, 1, 0))
    xp = jnp.where(last, 0.0, pltpu.roll(x, n - 1, 0))
    y = w_ref[0:1, :] * xm + w_ref[1:2, :] * x + w_ref[2:3, :] * xp + b_ref[...]
    if act:
        y = _silu(y)
    if len(o_refs) == 1:
        o_refs[0][...] = y
    else:
        bounds = starts + (n,)
        for o_ref, lo, hi in zip(o_refs, bounds[:-1], bounds[1:]):
            o_ref[...] = y[lo:hi]


def conv3(x, w, b, act, starts=(0,), col0=0, split=False, tc=LANES):
    n, c = x.shape[0], w.shape[1]
    off = col0 // tc
    starts = tuple(s for s in starts if s < n)
    sizes = [hi - lo for lo, hi in zip(starts, starts[1:] + (n,))] if split else [n]
    res = pl.pallas_call(
        functools.partial(_conv3_kernel, act=act, starts=starts),
        grid=(c // tc,),
        in_specs=[pl.BlockSpec((n, tc), lambda j: (0, j + off)),
                  pl.BlockSpec((3, tc), lambda j: (0, j)),
                  pl.BlockSpec((1, tc), lambda j: (0, j))],
        out_specs=[pl.BlockSpec((s, tc), lambda j: (0, j)) for s in sizes],
        out_shape=[jax.ShapeDtypeStruct((s, c), F32) for s in sizes],
        compiler_params=_cp("parallel"),
        name="conv3",
    )(x, w, b.reshape(1, c))
    return res if split else res[0]


def _hyfilt_kernel(fv_ref, w1_ref, b1_ref, fr1_ref, w2_ref, b2_ref, fr2_ref, w3_ref, dl_ref,
                   k_ref, as_ref, *, seq, nfull, tj):
    i = pl.program_id(0)
    j = i * tj + lax.broadcasted_iota(jnp.int32, (tj, 1), 0)
    valid = (j < seq) | (j > nfull - seq)
    tf = jnp.where(j < seq, j, nfull - j).astype(F32)
    t_lin = tf / (seq - 1)
    wpos = (2.0 * math.pi) * tf / seq
    lane = lax.broadcasted_iota(jnp.int32, (tj, LANES), 1)
    arg = wpos * fv_ref[...]
    feats = jnp.where(lane == 0, t_lin,
                      jnp.where(lane <= HY_BANDS, jnp.cos(arg),
                                jnp.where(lane <= 2 * HY_BANDS, -jnp.sin(arg), 0.0)))
    h = jnp.sin(fr1_ref[...] * (jnp.dot(feats, w1_ref[...], precision=HI, preferred_element_type=F32) + b1_ref[...]))
    h = jnp.sin(fr2_ref[...] * (jnp.dot(h, w2_ref[...], precision=HI, preferred_element_type=F32) + b2_ref[...]))
    h = jnp.dot(h.astype(BF16), w3_ref[0].astype(BF16), preferred_element_type=F32)
    kv = jnp.where(valid, h * jnp.exp(-t_lin * dl_ref[...]), 0.0)
    k_ref[...] = kv.astype(k_ref.dtype)

    @pl.when(i == 0)
    def _():
        as_ref[...] = jnp.zeros_like(as_ref)

    as_ref[0:1, :] += jnp.sum(jnp.abs(kv), axis=0, keepdims=True)


def hyena_filter(seq, nfull, w1, b1, fr1, w2, b2, fr2, w3):
    tj = min(512, nfull)
    nblk = nfull // tj
    f = jnp.linspace(1e-4, HY_BANDS - 1, HY_BANDS, dtype=F32)
    fv = jnp.concatenate([jnp.zeros((1,), F32), f, f, jnp.zeros((LANES - 2 * HY_BANDS - 1,), F32)]).reshape(1, LANES)
    w1p = jnp.pad(w1.astype(F32), ((0, LANES - w1.shape[0]), (0, 0)))
    w3r = w3.astype(F32).reshape(HY_FFN, 2, 2, HY_C).transpose(2, 0, 1, 3).reshape(2, HY_FFN, 2 * HY_C)
    deltas = jnp.abs(jnp.linspace(math.log(1e-2) / 1.5, math.log(1e-2) / 0.3, HY_C, dtype=F32))
    dl = jnp.concatenate([deltas, deltas]).reshape(1, 2 * HY_C)
    small = lambda r, c: pl.BlockSpec((r, c), lambda i: (0, 0))
    return pl.pallas_call(
        functools.partial(_hyfilt_kernel, seq=seq, nfull=nfull, tj=tj),
        grid=(nblk,),
        in_specs=[small(1, LANES), small(LANES, HY_FFN), small(1, HY_FFN), small(1, HY_FFN),
                  small(HY_FFN, HY_FFN), small(1, HY_FFN), small(1, HY_FFN),
                  pl.BlockSpec((1, HY_FFN, 2 * HY_C), lambda i: ((2 * i) // nblk, 0, 0)),
                  small(1, 2 * HY_C)],
        out_specs=[pl.BlockSpec((tj, 2 * HY_C), lambda i: (i, 0)),
                   pl.BlockSpec((8, 2 * HY_C), lambda i: (0, 0))],
        out_shape=[jax.ShapeDtypeStruct((nfull, 2 * HY_C), BF16),
                   jax.ShapeDtypeStruct((8, 2 * HY_C), F32)],
        compiler_params=_cp("arbitrary"),
        name="hyena_filter",
    )(fv, w1p, b1.reshape(1, -1).astype(F32), fr1.reshape(1, -1).astype(F32), w2.astype(F32),
      b2.reshape(1, -1).astype(F32), fr2.reshape(1, -1).astype(F32), w3r, dl)


def _spec_rows(n1):
    return -(-(n1 // 2 + 1) // 16) * 16


def _dft_tables(n1, rows_in, rows_out):
    nfull = n1 * LANES
    nr = _spec_rows(n1)
    k1 = np.arange(nr)[:, None]
    keep = (k1 <= n1 // 2).astype(np.float64)
    a = np.arange(n1)[None, :]
    th = 2.0 * np.pi * k1 * a / n1
    frow = np.concatenate([np.cos(th) * keep, -np.sin(th) * keep], axis=0)[:, :rows_in]
    b = np.arange(LANES)[None, :]
    tw = 2.0 * np.pi * k1 * b / nfull
    twr, twi = np.cos(tw), -np.sin(tw)
    ph = 2.0 * np.pi * np.arange(LANES)[:, None] * np.arange(LANES)[None, :] / LANES
    fr, fi = np.cos(ph), -np.sin(ph)
    f2 = np.block([[fr, fi], [-fi, fr]])
    f2i = np.block([[fr, -fi], [fi, fr]])
    kk = np.arange(nr)[None, :]
    wgt = np.where((kk == 0) | (kk == n1 // 2), 1.0, np.where(kk < n1 // 2, 2.0, 0.0))
    thi = 2.0 * np.pi * np.arange(rows_out)[:, None] * kk / n1
    finv = np.concatenate([np.cos(thi) * wgt, -np.sin(thi) * wgt], axis=1) / nfull
    c = lambda m, dt: jnp.asarray(m, dtype=dt)
    return dict(frow=c(frow, BF16), twr=c(twr, F32), twi=c(twi, F32), f2=c(f2, BF16), f2i=c(f2i, BF16),
                finv=c(finv, BF16))


def _hy_forward(z_bf, frow_ref, twr_ref, twi_ref, pbuf, qbuf, *, cb, n1, scale=None):
    p = jnp.dot(frow_ref[...], z_bf, preferred_element_type=F32)
    pbuf[...] = p if scale is None else p * scale
    twr, twi = twr_ref[...], twi_ref[...]
    for c in range(cb):
        pr = pbuf[0:n1, c * LANES:(c + 1) * LANES]
        pi = pbuf[n1:2 * n1, c * LANES:(c + 1) * LANES]
        qbuf[c * n1:(c + 1) * n1, 0:LANES] = (pr * twr - pi * twi).astype(BF16)
        qbuf[c * n1:(c + 1) * n1, LANES:2 * LANES] = (pr * twi + pi * twr).astype(BF16)


def _hyspec_kernel(k_ref, sc_ref, frow_ref, twr_ref, twi_ref, f2_ref, o_ref, pbuf, qbuf, *, cb, n1):
    _hy_forward(k_ref[...], frow_ref, twr_ref, twi_ref, pbuf, qbuf, cb=cb, n1=n1, scale=sc_ref[...])
    o_ref[...] = jnp.dot(qbuf[...], f2_ref[...], preferred_element_type=F32).astype(o_ref.dtype)


def hyena_spectrum(kmat, scale, tabs, n1, cb):
    nch = kmat.shape[1] // LANES
    nr = _spec_rows(n1)
    full = lambda s: pl.BlockSpec(s, lambda i: (0, 0))
    return pl.pallas_call(
        functools.partial(_hyspec_kernel, cb=cb, n1=nr),
        grid=(nch // cb,),
        in_specs=[pl.BlockSpec((n1, cb * LANES), lambda i: (0, i)),
                  pl.BlockSpec((1, cb * LANES), lambda i: (0, i)),
                  full((2 * nr, n1)), full((nr, LANES)), full((nr, LANES)), full((2 * LANES, 2 * LANES))],
        out_specs=pl.BlockSpec((cb * nr, 2 * LANES), lambda i: (i, 0)),
        out_shape=jax.ShapeDtypeStruct((nch * nr, 2 * LANES), BF16),
        scratch_shapes=[pltpu.VMEM((2 * nr, cb * LANES), F32), pltpu.VMEM((cb * nr, 2 * LANES), BF16)],
        compiler_params=_cp("parallel"),
        name="hyena_spectrum",
    )(kmat, scale, tabs["frow_full"], tabs["twr"], tabs["twi"], tabs["f2"])


def _hyconv_kernel(z_ref, g_ref, ks_ref, bias_ref, frow_ref, twr_ref, twi_ref, f2_ref, f2i_ref, finv_ref,
                   o_ref, pbuf, qbuf, sbuf, *, cb, n1, sub):
    z = z_ref[...]
    _hy_forward(z.astype(BF16), frow_ref, twr_ref, twi_ref, pbuf, qbuf, cb=cb, n1=n1)
    twr, twi = twr_ref[...], twi_ref[...]
    rows = sub * n1
    for s in range(cb // sub):
        rs = slice(s * rows, (s + 1) * rows)
        zz = jnp.dot(qbuf[rs, :], f2_ref[...], preferred_element_type=F32)
        zr, zi = zz[:, 0:LANES], zz[:, LANES:]
        kr, ki = ks_ref[rs, 0:LANES].astype(F32), ks_ref[rs, LANES:2 * LANES].astype(F32)
        y = jnp.concatenate([zr * kr - zi * ki, zr * ki + zi * kr], axis=1).astype(BF16)
        rr = jnp.dot(y, f2i_ref[...], preferred_element_type=F32)
        for cc in range(sub):
            c = s * sub + cc
            r_re = rr[cc * n1:(cc + 1) * n1, 0:LANES]
            r_im = rr[cc * n1:(cc + 1) * n1, LANES:]
            sbuf[0:n1, c * LANES:(c + 1) * LANES] = (r_re * twr + r_im * twi).astype(BF16)
            sbuf[n1:2 * n1, c * LANES:(c + 1) * LANES] = (r_im * twr - r_re * twi).astype(BF16)
    y = jnp.dot(finv_ref[...], sbuf[...], preferred_element_type=F32)
    o_ref[...] = (y + z * bias_ref[...]) * g_ref[...]


def hyena_conv(z, gate, kspec, order, bias, tabs, n1, cb, sub):
    rows = z[0].shape[0]
    nch, width = HY_C, HY_C * LANES
    nr = _spec_rows(n1)
    koff = order * (nch // cb)
    full = lambda s: pl.BlockSpec(s, lambda i: (0, 0))
    io = pl.BlockSpec((rows, cb * LANES), lambda i: (0, i))
    sel = lambda a: pl.BlockSpec((rows, cb * LANES), lambda i: (0, i + a[1] // cb))
    return pl.pallas_call(
        functools.partial(_hyconv_kernel, cb=cb, n1=nr, sub=sub),
        grid=(nch // cb,),
        in_specs=[sel(z), sel(gate),
                  pl.BlockSpec((cb * nr, 2 * LANES), lambda i: (i + koff, 0)),
                  pl.BlockSpec((1, cb * LANES), lambda i: (0, i)),
                  full((2 * nr, rows)), full((nr, LANES)), full((nr, LANES)),
                  full((2 * LANES, 2 * LANES)), full((2 * LANES, 2 * LANES)), full((rows, 2 * nr))],
        out_specs=io,
        out_shape=jax.ShapeDtypeStruct((rows, width), F32),
        scratch_shapes=[pltpu.VMEM((2 * nr, cb * LANES), F32), pltpu.VMEM((cb * nr, 2 * LANES), BF16),
                        pltpu.VMEM((2 * nr, cb * LANES), BF16)],
        compiler_params=_cp("parallel"),
        name="hyena_conv",
    )(z[0], gate[0], kspec, bias, tabs["frow"], tabs["twr"], tabs["twi"], tabs["f2"], tabs["f2i"], tabs["finv"])


def _to_ab(x, rows):
    n, c = x.shape
    y = x.reshape(n // LANES, LANES, c).transpose(0, 2, 1).reshape(n // LANES, c * LANES)
    return jnp.pad(y, ((0, rows - n // LANES), (0, 0)))


def _from_ab(y, n):
    c = y.shape[1] // LANES
    return y[:n // LANES].reshape(n // LANES, c, LANES).transpose(0, 2, 1).reshape(n, c)


def hyena_mixer(xc, filt, bias):
    seq = xc.shape[0]
    if seq >= 1024:
        n1, rows, cb, sub = 2 * seq // LANES, seq // LANES, 16, 2
    else:
        n1, rows, cb, sub = 16, 16, 128, 16
    nfull = n1 * LANES
    tabs = _dft_tables(n1, rows, rows)
    tabs["frow_full"] = _dft_tables(n1, n1, rows)["frow"]
    kfull, asum = hyena_filter(seq, nfull, *filt)
    scale = jnp.repeat(1.0 / (asum[0] + EPS), LANES).reshape(1, -1)
    kspec = hyena_spectrum(_to_ab(kfull, n1), scale, tabs, n1, cb)
    xab = _to_ab(xc, rows)
    bias_l = jnp.repeat(bias.astype(F32), LANES, axis=1)
    z = hyena_conv((xab, 2 * HY_C), (xab, 0), kspec, 0, bias_l[0:1], tabs, n1, cb, sub)
    y = hyena_conv((z, 0), (xab, HY_C), kspec, 1, bias_l[1:2], tabs, n1, cb, sub)
    return _from_ab(y, seq)


def _dscan_kernel(*refs, rev, n_groups, upg, hpu, wd, use_kap):
    if use_kap:
        q_ref, k_ref, v_ref, la_ref, kap_ref, y_ref, st_ref = refs
    else:
        q_ref, k_ref, v_ref, la_ref, y_ref, st_ref = refs
    i = pl.program_id(0)

    @pl.when(i == 0)
    def _():
        st_ref[...] = jnp.zeros_like(st_ref)

    c = q_ref.shape[0]
    row = lax.broadcasted_iota(jnp.int32, (c, c), 0)
    col = lax.broadcasted_iota(jnp.int32, (c, c), 1)
    mask = (row <= col) if rev else (row >= col)
    tri = jnp.where(mask, 1.0, 0.0).astype(BF16)
    cum = _cumsum_rows(tri, la_ref[...])
    cum_t = cum.T
    c_end = cum[0:1, :] if rev else cum[c - 1:c, :]
    e_all = jnp.exp(cum)
    w_all = jnp.exp(c_end - cum)
    e_end = jnp.exp(c_end)
    if use_kap:
        kap = kap_ref[...]
        w_all = w_all * kap
    lane = lax.broadcasted_iota(jnp.int32, (c, wd), 1)
    lane1 = lax.broadcasted_iota(jnp.int32, (1, wd), 1)
    sub = wd // hpu

    def lanesel(cols, ln):
        out = cols[-1]
        for hh in range(hpu - 2, -1, -1):
            out = jnp.where(ln < (hh + 1) * sub, cols[hh], out)
        return out

    for g in range(n_groups):
        qg = q_ref[:, g * LANES:(g + 1) * LANES].astype(BF16)
        kg = k_ref[:, g * LANES:(g + 1) * LANES]
        gm = lax.dot_general(qg, kg.astype(BF16), (((1,), (1,)), ((), ())), preferred_element_type=F32)
        kg_t = kg.T.astype(BF16)
        for uu in range(upg):
            u = g * upg + uu
            v = v_ref[:, u * wd:(u + 1) * wd]
            heads = [u * hpu + hh for hh in range(hpu)]
            st = st_ref[u]
            y = lanesel([e_all[:, h:h + 1] for h in heads], lane) * jnp.dot(
                qg, st.astype(BF16), preferred_element_type=F32)
            for hh, h in enumerate(heads):
                dmat = jnp.exp(jnp.where(mask, cum[:, h:h + 1] - cum_t[h:h + 1, :], NEG))
                vh = v * kap[:, h:h + 1] if use_kap else v
                if hpu > 1:
                    vh = jnp.where((lane >= hh * sub) & (lane < (hh + 1) * sub), vh, 0.0)
                y = y + jnp.dot((gm * dmat).astype(BF16), vh.astype(BF16), preferred_element_type=F32)
            y_ref[:, u * wd:(u + 1) * wd] = y
            vw = v * lanesel([w_all[:, h:h + 1] for h in heads], lane)
            st_ref[u] = (lanesel([e_end[:, h:h + 1] for h in heads], lane1) * st
                         + jnp.dot(kg_t, vw.astype(BF16), preferred_element_type=F32))


def _chunk_order(n, n_lat, c, rev):
    nl, nt = n_lat // c, n // c
    if rev:
        return lambda i: nt - 1 - i
    return lambda i: jnp.where(i < nt - nl, nl + i, i - (nt - nl))


def decay_scan(q, k, v, la, kap, n_lat, *, rev, n_groups, upg, hpu, wd):
    n = q[0].shape[0]
    c = SCAN_CHUNK
    n_units = n_groups * upg
    rmap = _chunk_order(n, n_lat, c, rev)

    def spec(width, blk):
        return pl.BlockSpec((c, width), lambda i: (rmap(i), blk))

    ins = [q, k, v, la] + ([kap] if kap is not None else [])
    widths = [n_groups * LANES, n_groups * LANES, n_units * wd, LANES] + ([LANES] if kap is not None else [])
    return pl.pallas_call(
        functools.partial(_dscan_kernel, rev=rev, n_groups=n_groups, upg=upg, hpu=hpu, wd=wd,
                          use_kap=kap is not None),
        grid=(n // c,),
        in_specs=[spec(w, a[1]) for w, a in zip(widths, ins)],
        out_specs=pl.BlockSpec((c, n_units * wd), lambda i: (rmap(i), 0)),
        out_shape=jax.ShapeDtypeStruct((n, n_units * wd), F32),
        scratch_shapes=[pltpu.VMEM((n_units, LANES, wd), F32)],
        compiler_params=_cp("arbitrary"),
        name="decay_scan_rev" if rev else "decay_scan_fwd",
    )(*[a[0] for a in ins])


def _post_kernel(*refs, n_terms, skip, pre_gate, ngroups):
    terms = refs[:n_terms]
    pos = n_terms
    y = terms[0][...]
    for t in terms[1:]:
        y = y + t[...]
    if skip:
        y = y + refs[pos][...] * refs[pos + 1][...]
        pos += 2
    gate_ref, ng_ref, o_ref = refs[pos:pos + 3]
    gate = _silu(gate_ref[...])
    if pre_gate:
        y = y * gate
    gw = y.shape[1] // ngroups
    for g in range(ngroups):
        sl = slice(g * gw, (g + 1) * gw)
        yg = y[:, sl]
        o = yg * lax.rsqrt(jnp.mean(yg * yg, axis=-1, keepdims=True) + EPS) * ng_ref[:, sl]
        if not pre_gate:
            o = o * gate[:, sl]
        o_ref[:, sl] = o.astype(o_ref.dtype)


def mixer_post(terms, skip, gate, norm_g, *, pre_gate, ngroups, width=1024, tm=256):
    n = terms[0][0].shape[0]
    tm = min(tm, n)
    blk = lambda a: pl.BlockSpec((tm, width), lambda i: (i, a[1]))
    vec = pl.BlockSpec((1, width), lambda i: (0, 0))
    arrays = [t[0] for t in terms]
    specs = [blk(t) for t in terms]
    if skip is not None:
        arrays += [skip[0][0], skip[1].reshape(1, width)]
        specs += [blk(skip[0]), vec]
    arrays += [gate[0], norm_g.reshape(1, width)]
    specs += [blk(gate), vec]
    return pl.pallas_call(
        functools.partial(_post_kernel, n_terms=len(terms), skip=skip is not None, pre_gate=pre_gate,
                          ngroups=ngroups),
        grid=(n // tm,),
        in_specs=specs,
        out_specs=pl.BlockSpec((tm, width), lambda i: (i, 0)),
        out_shape=jax.ShapeDtypeStruct((n, width), BF16),
        compiler_params=_cp("parallel"),
        name="mixer_post",
    )(*arrays)


def _ssd_dt_kernel(x_ref, b_ref, a_ref, dt_ref, la_ref):
    x = x_ref[...] + b_ref[...]
    dt = jnp.maximum(x, 0.0) + jnp.log1p(jnp.exp(-jnp.abs(x)))
    dt_ref[...] = dt
    la_ref[...] = dt * a_ref[...]


def ssd_mixer(u, dt_raw, conv_w, conv_b, dt_bias, a_log, d_skip, norm_g, n_lat):
    n = u.shape[0]
    xbc = conv3(u, conv_w, conv_b, True, starts=(0, n_lat), col0=SSD_INNER)
    pad = lambda t: jnp.pad(t.astype(F32), ((0, 0), (0, LANES - SSD_HEADS))).reshape(1, 2 * LANES)
    tm = 256
    dt, la = pl.pallas_call(
        _ssd_dt_kernel,
        grid=(n // tm,),
        in_specs=[pl.BlockSpec((tm, 2 * LANES), lambda i: (i, 0)),
                  pl.BlockSpec((1, 2 * LANES), lambda i: (0, 0)),
                  pl.BlockSpec((1, 2 * LANES), lambda i: (0, 0))],
        out_specs=[pl.BlockSpec((tm, 2 * LANES), lambda i: (i, 0))] * 2,
        out_shape=[jax.ShapeDtypeStruct((n, 2 * LANES), F32)] * 2,
        compiler_params=_cp("parallel"),
        name="ssd_dt",
    )(dt_raw, pad(dt_bias), pad(-jnp.exp(a_log.astype(F32))))
    ys = [(decay_scan((xbc, 3), (xbc, 2), (xbc, 0), (la, d), (dt, d), n_lat,
                      rev=bool(d), n_groups=SSD_GROUPS, upg=2, hpu=2, wd=LANES), 0) for d in range(2)]
    dsk = jnp.repeat(d_skip.astype(F32), SSD_P)
    return mixer_post(ys, ((xbc, 0), dsk), (u, 0), norm_g, pre_gate=True, ngroups=SSD_GROUPS)


def _rope_kernel(q_ref, k_ref, inv_ref, qo_ref, ko_ref, *, n_lat, n_ctx):
    tm = q_ref.shape[0]
    row = pl.program_id(0) * tm + lax.broadcasted_iota(jnp.int32, (tm, 1), 0)
    pos = jnp.where(row < n_lat, row + n_ctx, row - n_lat).astype(F32)
    ang = pos * inv_ref[...]
    cs, sn = jnp.cos(ang), jnp.sin(ang)
    lane = lax.broadcasted_iota(jnp.int32, (tm, LANES), 1)
    sgn = jnp.where(lane < RET_DK // 2, -sn, sn)
    for h in range(RET_HEADS):
        sl = slice(h * RET_DK, (h + 1) * RET_DK)
        for src, dst, scale in ((q_ref, qo_ref, 1.0), (k_ref, ko_ref, RET_DK ** -0.5)):
            x = src[:, sl]
            y = x * cs + pltpu.roll(x, RET_DK // 2, 1) * sgn
            dst[:, sl] = y * scale


def retention_mixer(u, norm_g, n_lat):
    n = u.shape[0]
    tm = 256
    half = RET_DK // 2
    inv = ROPE_BASE ** (-jnp.arange(half, dtype=F32) / half)
    inv2 = jnp.concatenate([inv, inv]).reshape(1, LANES)
    wq = RET_HEADS * RET_DK
    qr, kr = pl.pallas_call(
        functools.partial(_rope_kernel, n_lat=n_lat, n_ctx=n - n_lat),
        grid=(n // tm,),
        in_specs=[pl.BlockSpec((tm, wq), lambda i: (i, 0)), pl.BlockSpec((tm, wq), lambda i: (i, 1)),
                  pl.BlockSpec((1, LANES), lambda i: (0, 0))],
        out_specs=[pl.BlockSpec((tm, wq), lambda i: (i, 0))] * 2,
        out_shape=[jax.ShapeDtypeStruct((n, wq), F32)] * 2,
        compiler_params=_cp("parallel"),
        name="rope",
    )(u, u, inv2)
    log_gamma = jnp.log1p(-jnp.exp2(-5.0 - jnp.arange(RET_HEADS, dtype=F32)))
    la = jnp.broadcast_to(jnp.pad(log_gamma, (0, LANES - RET_HEADS)).reshape(1, LANES), (n, LANES))
    ys = [(decay_scan((qr, 0), (kr, 0), (u, 1), (la, 0), None, n_lat,
                      rev=bool(d), n_groups=RET_HEADS, upg=1, hpu=1, wd=RET_DV), 0) for d in range(2)]
    return mixer_post(ys, None, (u, 2), norm_g, pre_gate=False, ngroups=RET_HEADS)


def _anchor_rows(gc, b, rev):
    c, n = gc.shape
    shift = b if rev else b - 1
    if 2 * b >= 8:
        return jnp.concatenate([jnp.broadcast_to(gc[s + shift:s + shift + 1, :], (2 * b, n))
                                for s in range(0, c, 2 * b)], axis=0)
    sub = lax.broadcasted_iota(jnp.int32, (8, n), 0)
    tiles = []
    for t in range(0, c, 8):
        tile = None
        for m in range(8 // (2 * b)):
            r = t + m * 2 * b + shift
            cand = jnp.broadcast_to(gc[r:r + 1, :], (8, n))
            tile = cand if tile is None else jnp.where(sub >= m * 2 * b, cand, tile)
        tiles.append(tile)
    return jnp.concatenate(tiles, axis=0)


def _hg_kernel(q_ref, f_ref, v_ref, lb_ref, y_ref, st_ref, *, rev):
    i = pl.program_id(0)

    @pl.when(i == 0)
    def _():
        st_ref[...] = jnp.zeros_like(st_ref)

    c, width = q_ref.shape
    nh = width // HG_DK
    lb = lb_ref[...]
    qh = _silu(q_ref[...])
    fr = f_ref[...]
    lg = jnp.log(lb + (1.0 - lb) * jax.nn.sigmoid(fr))
    kk = (1.0 - lb) * jax.nn.sigmoid(-fr)
    row = lax.broadcasted_iota(jnp.int32, (c, c), 0)
    col = lax.broadcasted_iota(jnp.int32, (c, c), 1)
    tri = jnp.where((row <= col) if rev else (row >= col), 1.0, 0.0).astype(BF16)
    gc = _cumsum_rows(tri, lg)
    g_end = gc[0:1, :] if rev else gc[c - 1:c, :]
    qe = (qh * jnp.exp(gc)).astype(BF16)
    kw = (kk * jnp.exp(g_end - gc)).astype(BF16)
    dec = jnp.exp(g_end)
    vb = v_ref[...].astype(BF16)
    qb, kb = qh.astype(BF16), kk.astype(BF16)
    nt = (((1,), (1,)), ((), ()))
    hs = [slice(h * HG_DK, (h + 1) * HG_DK) for h in range(nh)]
    scores = [jnp.where(row == col, lax.dot_general(qb[:, s], kb[:, s], nt, preferred_element_type=F32), 0.0)
              for s in hs]
    ridx = lax.broadcasted_iota(jnp.int32, (c, 1), 0)
    lvl = 0
    while (1 << lvl) < c:
        b = 1 << lvl
        anchor = _anchor_rows(gc, b, rev)
        upper = ((ridx >> lvl) & 1) == 1
        q_on, k_on = (~upper, upper) if rev else (upper, ~upper)
        qt = (qh * jnp.exp(jnp.where(q_on, gc - anchor, NEG))).astype(BF16)
        kt = (kk * jnp.exp(jnp.where(k_on, anchor - gc, NEG))).astype(BF16)
        same = (row >> (lvl + 1)) == (col >> (lvl + 1))
        for h, s in enumerate(hs):
            sc = lax.dot_general(qt[:, s], kt[:, s], nt, preferred_element_type=F32)
            scores[h] = scores[h] + jnp.where(same, sc, 0.0)
        lvl += 1
    for h, s in enumerate(hs):
        st = st_ref[h]
        y = jnp.dot(scores[h].astype(BF16), vb[:, s], preferred_element_type=F32)
        y = y + lax.dot_general(qe[:, s], st.astype(BF16), nt, preferred_element_type=F32)
        y_ref[:, s] = y
        st_ref[h] = dec[:, s] * st + lax.dot_general(vb[:, s], kw[:, s], (((0,), (0,)), ((), ())),
                                                     preferred_element_type=F32)


def hgrn2_scan(u, lb, n_lat, *, rev, d):
    n = u.shape[0]
    c = HG_CHUNK
    rmap = _chunk_order(n, n_lat, c, rev)
    spec = lambda blk: pl.BlockSpec((c, HG_W), lambda i: (rmap(i), blk))
    return pl.pallas_call(
        functools.partial(_hg_kernel, rev=rev),
        grid=(n // c,),
        in_specs=[spec(0), spec(1 + d), spec(3), pl.BlockSpec((1, HG_W), lambda i: (0, 0))],
        out_specs=pl.BlockSpec((c, HG_W), lambda i: (rmap(i), 0)),
        out_shape=jax.ShapeDtypeStruct((n, HG_W), F32),
        scratch_shapes=[pltpu.VMEM((HG_HEADS, HG_DK, HG_DK), F32)],
        compiler_params=_cp("arbitrary"),
        name="hgrn2_scan_rev" if rev else "hgrn2_scan_fwd",
    )(u, u, u, lb)


def hgrn2_mixer(u, lb, norm_g, n_lat):
    ys = [(hgrn2_scan(u, lb[d:d + 1].astype(F32), n_lat, rev=bool(d), d=d), 0) for d in range(2)]
    return mixer_post(ys, None, (u, 4), norm_g, pre_gate=False, ngroups=HG_HEADS)


_HIGH16 = 0xFFFF0000


def _pack_rows(y, o_ref):
    t, d = y.shape
    nj = d // (2 * LANES)
    words = []
    for j in range(nj):
        lo = y[:, 2 * j * LANES:(2 * j + 1) * LANES].astype(BF16).astype(F32)
        hi = y[:, (2 * j + 1) * LANES:(2 * j + 2) * LANES].astype(BF16).astype(F32)
        words.append((pltpu.bitcast(lo, jnp.uint32) >> 16) | (pltpu.bitcast(hi, jnp.uint32) & jnp.uint32(_HIGH16)))
    o_ref[...] = jnp.swapaxes(jnp.stack(words, axis=0), 0, 1).reshape(t * nj, LANES)


def _unpack_rows(p_ref, t, nj):
    return jnp.swapaxes(p_ref[...].reshape(t, nj, LANES), 0, 1)


def _unpack_piece(words, j):
    lo = pltpu.bitcast(words[j] << 16, F32)
    hi = pltpu.bitcast(words[j] & jnp.uint32(_HIGH16), F32)
    return lo, hi


def _router_kernel(x_ref, g_ref, sc_ref, sh_ref, wr_ref, br_ref, h_ref, r_ref, *, n_lat):
    x = x_ref[...]
    tm = x.shape[0]
    y = x * lax.rsqrt(jnp.mean(x * x, axis=-1, keepdims=True) + EPS) * g_ref[...]
    h = y * (1.0 + _row_select(sc_ref, tm, n_lat)) + _row_select(sh_ref, tm, n_lat)
    _pack_rows(h, h_ref)
    lg = jnp.dot(h, wr_ref[...], precision=HI, preferred_element_type=F32) + br_ref[...]
    lane = lax.broadcasted_iota(jnp.int32, lg.shape, 1).astype(F32)
    big = float(LANES)
    is_g = lane < MOE_GROUPS
    gmax = jnp.max(jnp.where(is_g, lg, -jnp.inf), axis=1, keepdims=True)
    gsel = jnp.min(jnp.where(is_g & (lg == gmax), lane, big), axis=1, keepdims=True)
    pg = 1.0 / jnp.sum(jnp.where(is_g, jnp.exp(lg - gmax), 0.0), axis=1, keepdims=True)
    lo = MOE_GROUPS + MOE_PER_GROUP * gsel
    in_e = (lane >= lo) & (lane < lo + MOE_PER_GROUP)
    v1 = jnp.max(jnp.where(in_e, lg, -jnp.inf), axis=1, keepdims=True)
    i1 = jnp.min(jnp.where(in_e & (lg == v1), lane, big), axis=1, keepdims=True)
    rest = in_e & (lane != i1)
    v2 = jnp.max(jnp.where(rest, lg, -jnp.inf), axis=1, keepdims=True)
    i2 = jnp.min(jnp.where(rest & (lg == v2), lane, big), axis=1, keepdims=True)
    t = jnp.exp(v2 - v1)
    w1 = pg / (1.0 + t)
    w2 = pg * t / (1.0 + t)
    r_ref[...] = jnp.where(lane == 0, i1 - MOE_GROUPS,
                           jnp.where(lane == 1, i2 - MOE_GROUPS,
                                     jnp.where(lane == 2, w1, jnp.where(lane == 3, w2, 0.0))))


def moe_router(x, g, sc, sh, wr, br, m, n_lat, tm=256):
    d = x.shape[1]
    nj = d // (2 * LANES)
    vec = pl.BlockSpec((1, d), lambda i: (0, 0))
    two = pl.BlockSpec((2, d), lambda i: (0, 0))
    return pl.pallas_call(
        functools.partial(_router_kernel, n_lat=n_lat),
        grid=(m // tm,),
        in_specs=[pl.BlockSpec((tm, d), lambda i: (i, 0)), vec, two, two,
                  pl.BlockSpec((d, LANES), lambda i: (0, 0)), pl.BlockSpec((1, LANES), lambda i: (0, 0))],
        out_specs=[pl.BlockSpec((tm * nj, LANES), lambda i: (i, 0)), pl.BlockSpec((tm, LANES), lambda i: (i, 0))],
        out_shape=[jax.ShapeDtypeStruct((m * nj, LANES), jnp.uint32), jax.ShapeDtypeStruct((m, LANES), F32)],
        compiler_params=_cp("parallel"),
        name="moe_router",
    )(x, g.reshape(1, d), sc, sh, wr, br)


def _moe_plan(route, m, n_tiles):
    t = MOE_TILE
    pe = jnp.concatenate([route[:, 0], route[:, 1]]).astype(jnp.int32)
    onehot = (pe[:, None] == jnp.arange(MOE_EXPERTS, dtype=jnp.int32)[None, :]).astype(jnp.int32)
    csum = jnp.cumsum(onehot, axis=0)
    rank = jnp.take_along_axis(csum, pe[:, None], axis=1)[:, 0] - 1
    cnt = csum[-1]
    ntile = (cnt + t - 1) // t
    tend = jnp.cumsum(ntile)
    tstart = tend - ntile
    slot = tstart[pe] * t + rank
    tid = jnp.arange(n_tiles, dtype=jnp.int32)
    tile_e = jnp.clip(jnp.searchsorted(tend, tid, side="right"), 0, MOE_EXPERTS - 1).astype(jnp.int32)
    tile_valid = jnp.where(tid < tend[-1], jnp.clip(cnt[tile_e] - (tid - tstart[tile_e]) * t, 0, t), 0)
    tok = jnp.tile(jnp.arange(m, dtype=jnp.int32), 2)
    row_tok = jnp.zeros((n_tiles * t,), jnp.int32).at[slot].set(tok)
    return row_tok, tile_e, tile_valid.astype(jnp.int32), slot[:m], slot[m:]


def _row_copies(idx_ref, base, src_hbm, dst, sem, rows, nj, wait):
    if wait:
        pltpu.make_async_copy(src_hbm.at[pl.ds(0, rows * nj), :], dst, sem).wait()
        return

    def body(r, carry):
        src = idx_ref[base + r] * nj
        pltpu.make_async_copy(src_hbm.at[pl.ds(src, nj), :], dst.at[pl.ds(r * nj, nj), :], sem).start()
        return carry
    lax.fori_loop(0, rows, body, 0, unroll=8)


def _expert_kernel(te_ref, tv_ref, tok_ref, hp_hbm, w1_ref, w3_ref, w2_ref, o_ref, buf, xbuf, sem, *, nj):
    i = pl.program_id(0)
    n = pl.num_programs(0)
    t = xbuf.shape[0]

    def fetch(tile, slot):
        _row_copies(tok_ref, tile * t, hp_hbm, buf.at[slot], sem.at[slot], t, nj, wait=False)

    @pl.when((i == 0) & (tv_ref[0] > 0))
    def _():
        fetch(0, 0)

    @pl.when((i + 1 < n) & (tv_ref[jnp.minimum(i + 1, n - 1)] > 0))
    def _():
        fetch(i + 1, (i + 1) % 2)

    @pl.when(tv_ref[i] > 0)
    def _():
        slot = i % 2
        cur = buf.at[slot]
        _row_copies(tok_ref, 0, hp_hbm, cur, sem.at[slot], t, nj, wait=True)
        words = _unpack_rows(cur, t, nj)
        for j in range(nj):
            lo, hi = _unpack_piece(words, j)
            xbuf[:, 2 * j * LANES:(2 * j + 1) * LANES] = lo.astype(BF16)
            xbuf[:, (2 * j + 1) * LANES:(2 * j + 2) * LANES] = hi.astype(BF16)
        x = xbuf[...]
        a = jnp.dot(x, w1_ref[0, 0], preferred_element_type=F32)
        b = jnp.dot(x, w3_ref[0, 0], preferred_element_type=F32)
        y = jnp.dot((_silu(a) * b).astype(BF16), w2_ref[0, 0], preferred_element_type=F32)
        _pack_rows(y, o_ref)

    @pl.when(tv_ref[i] == 0)
    def _():
        o_ref[...] = jnp.zeros_like(o_ref)


def moe_experts(hp, row_tok, tile_e, tile_valid, w1, w3, w2, layer):
    t = MOE_TILE
    n_tiles = tile_e.shape[0]
    d, ff = w1.shape[2], w1.shape[3]
    nj = d // (2 * LANES)
    wmap = lambda i, te, tv, tok: (layer, te[i], 0, 0)
    return pl.pallas_call(
        functools.partial(_expert_kernel, nj=nj),
        grid_spec=pltpu.PrefetchScalarGridSpec(
            num_scalar_prefetch=3, grid=(n_tiles,),
            in_specs=[pl.BlockSpec(memory_space=pl.ANY),
                      pl.BlockSpec((1, 1, d, ff), wmap), pl.BlockSpec((1, 1, d, ff), wmap),
                      pl.BlockSpec((1, 1, ff, d), wmap)],
            out_specs=pl.BlockSpec((t * nj, LANES), lambda i, te, tv, tok: (i, 0)),
            scratch_shapes=[pltpu.VMEM((2, t * nj, LANES), jnp.uint32), pltpu.VMEM((t, d), BF16),
                            pltpu.SemaphoreType.DMA((2,))]),
        out_shape=jax.ShapeDtypeStruct((n_tiles * t * nj, LANES), jnp.uint32),
        compiler_params=_cp("arbitrary"),
        name="moe_experts",
    )(tile_e, tile_valid, row_tok, hp, w1, w3, w2)


def _combine_kernel(s0_ref, s1_ref, x_ref, r_ref, g_ref, fg_ref, ys_hbm, o_ref, buf, sem, *, n_lat, final, nj):
    tt = x_ref.shape[0]
    base = pl.program_id(0) * tt
    for k, s_ref in enumerate((s0_ref, s1_ref)):
        _row_copies(s_ref, base, ys_hbm, buf.at[k], sem.at[k], tt, nj, wait=False)
    for k, s_ref in enumerate((s0_ref, s1_ref)):
        _row_copies(s_ref, base, ys_hbm, buf.at[k], sem.at[k], tt, nj, wait=True)
    r = r_ref[...]
    w0 = jnp.broadcast_to(r[:, 2:3], (tt, LANES))
    w1 = jnp.broadcast_to(r[:, 3:4], (tt, LANES))
    is_ctx = base >= n_lat
    words0 = _unpack_rows(buf.at[0], tt, nj)
    words1 = _unpack_rows(buf.at[1], tt, nj)
    for j in range(nj):
        p0 = _unpack_piece(words0, j)
        p1 = _unpack_piece(words1, j)
        for half in range(2):
            cs = slice((2 * j + half) * LANES, (2 * j + half + 1) * LANES)
            gate = jnp.where(is_ctx, g_ref[1:2, cs], g_ref[0:1, cs])
            o_ref[:, cs] = x_ref[:, cs] + gate * (w0 * p0[half] + w1 * p1[half])
    if final:
        y = o_ref[...]
        o_ref[...] = y * lax.rsqrt(jnp.mean(y * y, axis=-1, keepdims=True) + EPS) * fg_ref[...]


def moe_combine(x, route, gate, final_g, ys, slot0, slot1, m, n_lat, final, tt=256):
    d = x.shape[1]
    nj = d // (2 * LANES)
    assert n_lat % tt == 0
    return pl.pallas_call(
        functools.partial(_combine_kernel, n_lat=n_lat, final=final, nj=nj),
        grid_spec=pltpu.PrefetchScalarGridSpec(
            num_scalar_prefetch=2, grid=(m // tt,),
            in_specs=[pl.BlockSpec((tt, d), lambda i, a, b: (i, 0)),
                      pl.BlockSpec((tt, LANES), lambda i, a, b: (i, 0)),
                      pl.BlockSpec((2, d), lambda i, a, b: (0, 0)),
                      pl.BlockSpec((1, d), lambda i, a, b: (0, 0)),
                      pl.BlockSpec(memory_space=pl.ANY)],
            out_specs=pl.BlockSpec((tt, d), lambda i, a, b: (i, 0)),
            scratch_shapes=[pltpu.VMEM((2, tt * nj, LANES), jnp.uint32), pltpu.SemaphoreType.DMA((2,))]),
        out_shape=jax.ShapeDtypeStruct((m, d), F32),
        compiler_params=_cp("arbitrary"),
        name="moe_combine",
    )(slot0, slot1, x, route, gate, final_g.reshape(1, d), ys)


def hier_moe(x, norm_g, sc, sh, gate, P, final_g, m, n_lat, final):
    n_tiles = 2 * m // MOE_TILE + MOE_EXPERTS
    hp, route = moe_router(x, norm_g, sc, sh, P["moe_wr"], P["moe_br"], m, n_lat)
    row_tok, tile_e, tile_valid, slot0, slot1 = _moe_plan(route, m, n_tiles)
    ys = moe_experts(hp, row_tok, tile_e, tile_valid, P["moe_w1"], P["moe_w3"], P["moe_w2"], P["layer"])
    return moe_combine(x, route, gate, final_g, ys, slot0, slot1, m, n_lat, final)


def _to_cols(t, rows):
    n, ch = t.shape
    return jnp.swapaxes(t.reshape(rows, GRID_W, ch), 0, 1).reshape(n, ch)


def _from_cols(t, rows):
    n, ch = t.shape
    return jnp.swapaxes(t.reshape(GRID_W, rows, ch), 0, 1).reshape(n, ch)


def _layer(xa, mod, P, lb, n_lat, last, final_g):
    n, d = xa.shape
    sh1, sc1, g1, sh2, sc2, g2 = (mod[:, q * d:(q + 1) * d] for q in range(6))
    grid_rows = n_lat // GRID_W
    h, h_lat = norm_modulate(xa, P["norm1"], sc1, sh1, n_lat)
    h_cols = jnp.concatenate([_to_cols(h_lat, grid_rows), h[n_lat:]], axis=0)
    tm = 768 if n % 768 == 0 else 256
    layer, w_in = P["layer"], P["w_in"]
    u_hy = matmul(h, w_in, layer, W_IN_HY, 3072, tm)
    u_ssd = matmul(h, w_in, layer, W_IN_SSD, 3072, tm)
    u_hg = matmul(h_cols, w_in, layer, W_IN_HG, 5120, tm)
    u_ret = matmul(h, w_in, layer, W_IN_RET, 3072, tm)
    dt_raw = matmul(h, P["w_dt"], layer, 0, 2 * LANES, tm, tn=2 * LANES)
    filt = (P["hy_w1"], P["hy_b1"], P["hy_fr1"], P["hy_w2"], P["hy_b2"], P["hy_fr2"], P["hy_w3"])
    xc_lat, xc_ctx = conv3(u_hy, P["hy_conv_w"], P["hy_conv_b"], False, starts=(0, n_lat), split=True)
    a = hyena_mixer(xc_lat, filt, P["hy_bias"]).astype(BF16)
    if not last:
        a = jnp.concatenate([a, hyena_mixer(xc_ctx, filt, P["hy_bias"]).astype(BF16)], axis=0)
    b = ssd_mixer(u_ssd, dt_raw, P["ssd_conv_w"], P["ssd_conv_b"], P["ssd_dt_bias"], P["ssd_a_log"],
                  P["ssd_d"], P["ssd_norm_g"], n_lat)
    c = hgrn2_mixer(u_hg, lb, P["hg_norm_g"], n_lat)
    c = jnp.concatenate([_from_cols(c[:n_lat], grid_rows), c[n_lat:]], axis=0)
    dd = retention_mixer(u_ret, P["ret_norm_g"], n_lat)
    m = n_lat if last else n
    xa = matmul_out((a, b, c, dd), P["w_out"], layer, xa, g1, m, n_lat, tm=1024 if m % 1024 == 0 else tm)
    return hier_moe(xa, P["norm2"], sc2, sh2, g2, P, final_g, m, n_lat, last)


def kernel(x, c, ctx, c_ctx, ada_w, ada_b, norm1_g, norm2_g, w_in, w_out, hy_conv_w, hy_conv_b, hy_w1, hy_b1, hy_fr1, hy_w2, hy_b2, hy_fr2, hy_w3, hy_bias, ssd_conv_w, ssd_conv_b, ssd_dt_bias, ssd_a_log, ssd_d, ssd_norm_g, hg_lb_raw, hg_norm_g, ret_norm_g, moe_wg, moe_bg, moe_we, moe_be, moe_w1, moe_w3, moe_w2, final_g):
    depth = ada_w.shape[0]
    n_lat = x.shape[1]
    lb_prob = jax.nn.softmax(hg_lb_raw.astype(F32), axis=0)
    lb_all = jnp.cumsum(lb_prob, axis=0) - lb_prob[0]
    mods = ada_modulation(c, c_ctx, ada_w, ada_b)
    xa = jnp.concatenate([x[0], ctx[0]], axis=0)
    o_dt, n_dt = 6144, 2 * SSD_HEADS
    w_in_b = regroup_w_in(w_in, o_dt, n_dt)
    zpad = jnp.zeros(w_in.shape[:2] + (LANES - SSD_HEADS,), w_in.dtype)
    w_dt_b = jnp.concatenate([w_in[:, :, o_dt:o_dt + SSD_HEADS], zpad,
                              w_in[:, :, o_dt + SSD_HEADS:o_dt + n_dt], zpad], axis=2).astype(BF16)
    w_out_b = w_out.astype(BF16)
    moe_b = [w.astype(BF16) for w in (moe_w1, moe_w3, moe_w2)]
    for l in range(depth):
        pad_r = LANES - MOE_GROUPS - MOE_EXPERTS
        P = {
            "layer": l, "norm1": norm1_g[l], "norm2": norm2_g[l], "w_in": w_in_b, "w_dt": w_dt_b, "w_out": w_out_b,
            "hy_conv_w": hy_conv_w[l], "hy_conv_b": hy_conv_b[l], "hy_w1": hy_w1[l], "hy_b1": hy_b1[l],
            "hy_fr1": hy_fr1[l], "hy_w2": hy_w2[l], "hy_b2": hy_b2[l], "hy_fr2": hy_fr2[l], "hy_w3": hy_w3[l],
            "hy_bias": hy_bias[l],
            "ssd_conv_w": ssd_conv_w[l], "ssd_conv_b": ssd_conv_b[l], "ssd_dt_bias": ssd_dt_bias[l],
            "ssd_a_log": ssd_a_log[l], "ssd_d": ssd_d[l], "ssd_norm_g": ssd_norm_g[l],
            "hg_norm_g": hg_norm_g[l], "ret_norm_g": ret_norm_g[l],
            "moe_wr": jnp.pad(jnp.concatenate([moe_wg[l], moe_we[l]], axis=1).astype(F32), ((0, 0), (0, pad_r))),
            "moe_br": jnp.pad(jnp.concatenate([moe_bg[l], moe_be[l]]).astype(F32), (0, pad_r)).reshape(1, LANES),
            "moe_w1": moe_b[0], "moe_w3": moe_b[1], "moe_w2": moe_b[2],
        }
        xa = _layer(xa, mods[l, 0:2], P, lb_all[l], n_lat, l == depth - 1, final_g)
    return xa[None]
```

```python
import functools
import math

import numpy as np
import jax
import jax.numpy as jnp
from jax import lax
from jax.experimental import pallas as pl
from jax.experimental.pallas import tpu as pltpu

F32 = jnp.float32
BF16 = jnp.bfloat16
EPS = 1e-6
NEG = -1e30
LANES = 128
VMEM_LIMIT = 56 * 1024 * 1024

D_MODEL = 4096
GRID_W = 64
HY_C = 1024
HY_BANDS = 16
HY_FFN = 64
SSD_HEADS = 16
SSD_P = 64
SSD_INNER = SSD_HEADS * SSD_P
SSD_GROUPS = 4
SSD_N = 128
HG_HEADS = 8
HG_DK = 128
HG_W = HG_HEADS * HG_DK
RET_HEADS = 4
RET_DK = 128
RET_DV = 256
RET_W = RET_HEADS * RET_DV
ROPE_BASE = 10000.0
MOE_GROUPS = 4
MOE_PER_GROUP = 8
MOE_EXPERTS = 32
MOE_FF = 512
MOE_TILE = 256
SCAN_CHUNK = 128
HG_CHUNK = 128
HI = lax.Precision.HIGHEST


def _cp(*sem):
    return pltpu.CompilerParams(dimension_semantics=sem, vmem_limit_bytes=VMEM_LIMIT)


def _silu(x):
    return x * jax.nn.sigmoid(x)


def _split3(x):
    hi = x.astype(BF16)
    r1 = x - hi.astype(F32)
    mid = r1.astype(BF16)
    lo = (r1 - mid.astype(F32)).astype(BF16)
    return hi, mid, lo


def _cumsum_rows(tri, x):
    hi, mid, lo = _split3(x)
    return (jnp.dot(tri, hi, preferred_element_type=F32)
            + jnp.dot(tri, mid, preferred_element_type=F32)
            + jnp.dot(tri, lo, preferred_element_type=F32))


def _ada_kernel(cb_ref, w_ref, b_ref, o_ref, acc_ref, *, nk):
    k = pl.program_id(2)

    @pl.when(k == 0)
    def _():
        acc_ref[...] = jnp.zeros_like(acc_ref)

    tk, tn = w_ref.shape[1], w_ref.shape[2]
    for s in range(2):
        sv = _silu(cb_ref[s])
        for j in range(tn // LANES):
            p = w_ref[0, :, j * LANES:(j + 1) * LANES] * sv
            acc_ref[s, :, j * LANES:(j + 1) * LANES] += p.reshape(tk // 8, 8, LANES).sum(axis=0)

    @pl.when(k == nk - 1)
    def _():
        r0 = acc_ref[0].sum(axis=0, keepdims=True) + b_ref[0]
        r1 = acc_ref[1].sum(axis=0, keepdims=True) + b_ref[0]
        row = lax.broadcasted_iota(jnp.int32, (8, tn), 0)
        o_ref[0] = jnp.where(row == 0, r0, jnp.where(row == 1, r1, 0.0))


def ada_modulation(c, c_ctx, ada_w, ada_b):
    depth, d, n = ada_w.shape
    tk, tn = min(512, d), math.gcd(n, 2048)
    cb = jnp.stack([jnp.broadcast_to(c.reshape(d, 1), (d, LANES)),
                    jnp.broadcast_to(c_ctx.reshape(d, 1), (d, LANES))])
    nk = d // tk
    return pl.pallas_call(
        functools.partial(_ada_kernel, nk=nk),
        grid=(depth, n // tn, nk),
        in_specs=[pl.BlockSpec((2, tk, LANES), lambda l, j, k: (0, k, 0)),
                  pl.BlockSpec((1, tk, tn), lambda l, j, k: (l, k, j)),
                  pl.BlockSpec((1, 1, tn), lambda l, j, k: (l, 0, j))],
        out_specs=pl.BlockSpec((1, 8, tn), lambda l, j, k: (l, 0, j)),
        out_shape=jax.ShapeDtypeStruct((depth, 8, n), F32),
        scratch_shapes=[pltpu.VMEM((2, 8, tn), F32)],
        compiler_params=_cp("arbitrary", "arbitrary", "arbitrary"),
        name="ada_modulation",
    )(cb, ada_w, ada_b.reshape(depth, 1, n))


def _row_select(ref, tm, n_lat):
    row = pl.program_id(0) * tm + lax.broadcasted_iota(jnp.int32, (tm, 1), 0)
    return jnp.where(row < n_lat, ref[0:1, :], ref[1:2, :])


def _norm_kernel(x_ref, g_ref, sc_ref, sh_ref, o_ref, lat_ref, *, n_lat):
    x = x_ref[...]
    tm = x.shape[0]
    y = x * lax.rsqrt(jnp.mean(x * x, axis=-1, keepdims=True) + EPS) * g_ref[...]
    h = (y * (1.0 + _row_select(sc_ref, tm, n_lat)) + _row_select(sh_ref, tm, n_lat)).astype(o_ref.dtype)
    o_ref[...] = h

    @pl.when(pl.program_id(0) * tm < n_lat)
    def _():
        lat_ref[...] = h


def norm_modulate(x, g, sc, sh, n_lat, out_dtype=BF16, tm=256):
    m, d = x.shape
    vec = pl.BlockSpec((1, d), lambda i: (0, 0))
    two = pl.BlockSpec((2, d), lambda i: (0, 0))
    last_lat = n_lat // tm - 1
    return pl.pallas_call(
        functools.partial(_norm_kernel, n_lat=n_lat),
        grid=(m // tm,),
        in_specs=[pl.BlockSpec((tm, d), lambda i: (i, 0)), vec, two, two],
        out_specs=[pl.BlockSpec((tm, d), lambda i: (i, 0)),
                   pl.BlockSpec((tm, d), lambda i: (jnp.minimum(i, last_lat), 0))],
        out_shape=[jax.ShapeDtypeStruct((m, d), out_dtype), jax.ShapeDtypeStruct((n_lat, d), out_dtype)],
        compiler_params=_cp("arbitrary"),
        name="norm_modulate",
    )(x, g.reshape(1, d), sc, sh)


def _mm_kernel(a_ref, w_ref, o_ref):
    o_ref[...] = jnp.dot(a_ref[...], w_ref[0], preferred_element_type=F32).astype(o_ref.dtype)


def matmul(a, w, layer, col0, n, tm, tn=512, out_dtype=F32):
    m, k = a.shape
    tn = min(tn, n)
    off = col0 // tn
    return pl.pallas_call(
        _mm_kernel,
        grid=(m // tm, n // tn),
        in_specs=[pl.BlockSpec((tm, k), lambda i, j: (i, 0)),
                  pl.BlockSpec((1, k, tn), lambda i, j: (layer, 0, j + off))],
        out_specs=pl.BlockSpec((tm, tn), lambda i, j: (i, j)),
        out_shape=jax.ShapeDtypeStruct((m, n), out_dtype),
        compiler_params=_cp("parallel", "parallel"),
        name="matmul",
    )(a, w)


def _mm_out_kernel(a0_ref, a1_ref, a2_ref, a3_ref, w_ref, r_ref, g_ref, o_ref, *, n_lat):
    tm, kq = a0_ref.shape
    acc = jnp.dot(a0_ref[...], w_ref[0, 0:kq, :], preferred_element_type=F32)
    for q, a_ref in enumerate((a1_ref, a2_ref, a3_ref), start=1):
        acc += jnp.dot(a_ref[...], w_ref[0, q * kq:(q + 1) * kq, :], preferred_element_type=F32)
    o_ref[...] = r_ref[...] + _row_select(g_ref, tm, n_lat) * acc


def matmul_out(parts, w, layer, res, gate, m, n_lat, tm, tn=512):
    kq = parts[0].shape[1]
    n = w.shape[2]
    tn = min(tn, n)
    a_spec = pl.BlockSpec((tm, kq), lambda i, j: (i, 0))
    return pl.pallas_call(
        functools.partial(_mm_out_kernel, n_lat=n_lat),
        grid=(m // tm, n // tn),
        in_specs=[a_spec, a_spec, a_spec, a_spec,
                  pl.BlockSpec((1, 4 * kq, tn), lambda i, j: (layer, 0, j)),
                  pl.BlockSpec((tm, tn), lambda i, j: (i, j)),
                  pl.BlockSpec((2, tn), lambda i, j: (0, j))],
        out_specs=pl.BlockSpec((tm, tn), lambda i, j: (i, j)),
        out_shape=jax.ShapeDtypeStruct((m, n), F32),
        compiler_params=_cp("parallel", "parallel"),
        name="matmul_out",
    )(*parts, w, res, gate)


def _conv3_kernel(x_ref, w_ref, b_ref, *o_refs, act, starts):
    x = x_ref[...]
    n = x.shape[0]
    row = lax.broadcasted_iota(jnp.int32, x.shape, 0)
    first = row == starts[0]
    last = row == n - 1
    for s in starts[1:]:
        first = first | (row == s)
        last = last | (row == s - 1)
    xm = jnp.where(first, 0.0, pltpu.roll(x, 1, 0))
    xp = jnp.where(last, 0.0, pltpu.roll(x, n - 1, 0))
    y = w_ref[0:1, :] * xm + w_ref[1:2, :] * x + w_ref[2:3, :] * xp + b_ref[...]
    if act:
        y = _silu(y)
    if len(o_refs) == 1:
        o_refs[0][...] = y
    else:
        bounds = starts + (n,)
        for o_ref, lo, hi in zip(o_refs, bounds[:-1], bounds[1:]):
            o_ref[...] = y[lo:hi]


def conv3(x, w, b, act, starts=(0,), col0=0, split=False, tc=LANES):
    n, c = x.shape[0], w.shape[1]
    off = col0 // tc
    starts = tuple(s for s in starts if s < n)
    sizes = [hi - lo for lo, hi in zip(starts, starts[1:] + (n,))] if split else [n]
    res = pl.pallas_call(
        functools.partial(_conv3_kernel, act=act, starts=starts),
        grid=(c // tc,),
        in_specs=[pl.BlockSpec((n, tc), lambda j: (0, j + off)),
                  pl.BlockSpec((3, tc), lambda j: (0, j)),
                  pl.BlockSpec((1, tc), lambda j: (0, j))],
        out_specs=[pl.BlockSpec((s, tc), lambda j: (0, j)) for s in sizes],
        out_shape=[jax.ShapeDtypeStruct((s, c), F32) for s in sizes],
        compiler_params=_cp("parallel"),
        name="conv3",
    )(x, w, b.reshape(1, c))
    return res if split else res[0]


def _hyfilt_kernel(fv_ref, w1_ref, b1_ref, fr1_ref, w2_ref, b2_ref, fr2_ref, w3_ref, dl_ref,
                   k_ref, as_ref, *, seq, nfull, tj):
    i = pl.program_id(0)

    def time_of(j):
        tf = jnp.where(j < seq, j, nfull - j).astype(F32)
        return (j < seq) | (j > nfull - seq), tf / (seq - 1), (2.0 * math.pi) * tf / seq

    _, t_row, wpos = time_of(i * tj + lax.broadcasted_iota(jnp.int32, (1, tj), 1))
    nf = fv_ref.shape[0]
    frow = lax.broadcasted_iota(jnp.int32, (nf, tj), 0)
    arg = fv_ref[:, 0:1] * wpos
    feats = jnp.where(frow == 0, t_row,
                      jnp.where(frow <= HY_BANDS, jnp.cos(arg),
                                jnp.where(frow <= 2 * HY_BANDS, -jnp.sin(arg), 0.0)))
    h = jnp.sin(fr1_ref[:, 0:1] * (jnp.dot(w1_ref[...], feats, precision=HI, preferred_element_type=F32)
                                   + b1_ref[:, 0:1]))
    h = jnp.sin(fr2_ref[:, 0:1] * (jnp.dot(w2_ref[...], h, precision=HI, preferred_element_type=F32)
                                   + b2_ref[:, 0:1]))
    h = lax.dot_general(h.astype(BF16), w3_ref[0].astype(BF16), (((0,), (0,)), ((), ())),
                        preferred_element_type=F32)
    valid, t_col, _ = time_of(i * tj + lax.broadcasted_iota(jnp.int32, (tj, 1), 0))
    kv = jnp.where(valid, h * jnp.exp(-t_col * dl_ref[...]), 0.0)
    k_ref[...] = kv.astype(k_ref.dtype)

    @pl.when(i == 0)
    def _():
        as_ref[...] = jnp.zeros_like(as_ref)

    as_ref[0:1, :] += jnp.sum(jnp.abs(kv), axis=0, keepdims=True)


def hyena_filter(seq, nfull, w1, b1, fr1, w2, b2, fr2, w3):
    tj = min(512, nfull)
    nblk = nfull // tj
    nf = 40
    f = jnp.linspace(1e-4, HY_BANDS - 1, HY_BANDS, dtype=F32)
    col = lambda v: jnp.broadcast_to(v.astype(F32).reshape(-1, 1), (v.shape[0], LANES))
    fv = col(jnp.concatenate([jnp.zeros((1,), F32), f, f, jnp.zeros((nf - 2 * HY_BANDS - 1,), F32)]))
    w1t = jnp.pad(w1.astype(F32).T, ((0, 0), (0, nf - w1.shape[0])))
    w3r = w3.astype(F32).reshape(HY_FFN, 2, 2, HY_C).transpose(2, 0, 1, 3).reshape(2, HY_FFN, 2 * HY_C)
    deltas = jnp.abs(jnp.linspace(math.log(1e-2) / 1.5, math.log(1e-2) / 0.3, HY_C, dtype=F32))
    dl = jnp.concatenate([deltas, deltas]).reshape(1, 2 * HY_C)
    small = lambda r, c: pl.BlockSpec((r, c), lambda i: (0, 0))
    return pl.pallas_call(
        functools.partial(_hyfilt_kernel, seq=seq, nfull=nfull, tj=tj),
        grid=(nblk,),
        in_specs=[small(nf, LANES), small(HY_FFN, nf), small(HY_FFN, LANES), small(HY_FFN, LANES),
                  small(HY_FFN, HY_FFN), small(HY_FFN, LANES), small(HY_FFN, LANES),
                  pl.BlockSpec((1, HY_FFN, 2 * HY_C), lambda i: ((2 * i) // nblk, 0, 0)),
                  small(1, 2 * HY_C)],
        out_specs=[pl.BlockSpec((tj, 2 * HY_C), lambda i: (i, 0)),
                   pl.BlockSpec((8, 2 * HY_C), lambda i: (0, 0))],
        out_shape=[jax.ShapeDtypeStruct((nfull, 2 * HY_C), BF16),
                   jax.ShapeDtypeStruct((8, 2 * HY_C), F32)],
        compiler_params=_cp("arbitrary"),
        name="hyena_filter",
    )(fv, w1t, col(b1), col(fr1), w2.astype(F32).T, col(b2), col(fr2), w3r, dl)


def _spec_rows(n1):
    return -(-(n1 // 2 + 1) // 16) * 16


def _dft_tables(n1, rows_in, rows_out):
    nfull = n1 * LANES
    nr = _spec_rows(n1)
    k1 = np.arange(nr)[:, None]
    keep = (k1 <= n1 // 2).astype(np.float64)
    a = np.arange(n1)[None, :]
    th = 2.0 * np.pi * k1 * a / n1
    frow = np.concatenate([np.cos(th) * keep, -np.sin(th) * keep], axis=0)[:, :rows_in]
    b = np.arange(LANES)[None, :]
    tw = 2.0 * np.pi * k1 * b / nfull
    twr, twi = np.cos(tw), -np.sin(tw)
    ph = 2.0 * np.pi * np.arange(LANES)[:, None] * np.arange(LANES)[None, :] / LANES
    fr, fi = np.cos(ph), -np.sin(ph)
    f2 = np.block([[fr, fi], [-fi, fr]])
    f2i = np.block([[fr, -fi], [fi, fr]])
    kk = np.arange(nr)[None, :]
    wgt = np.where((kk == 0) | (kk == n1 // 2), 1.0, np.where(kk < n1 // 2, 2.0, 0.0))
    thi = 2.0 * np.pi * np.arange(rows_out)[:, None] * kk / n1
    finv = np.concatenate([np.cos(thi) * wgt, -np.sin(thi) * wgt], axis=1) / nfull
    c = lambda m, dt: jnp.asarray(m, dtype=dt)
    return dict(frow=c(frow, BF16), twr=c(twr, F32), twi=c(twi, F32), f2=c(f2, BF16), f2i=c(f2i, BF16),
                finv=c(finv, BF16))


def _hy_forward(z_bf, frow_ref, twr_ref, twi_ref, pbuf, qbuf, *, cb, n1, scale=None):
    p = jnp.dot(frow_ref[...], z_bf, preferred_element_type=F32)
    pbuf[...] = p if scale is None else p * scale
    twr, twi = twr_ref[...], twi_ref[...]
    for c in range(cb):
        pr = pbuf[0:n1, c * LANES:(c + 1) * LANES]
        pi = pbuf[n1:2 * n1, c * LANES:(c + 1) * LANES]
        qbuf[c * n1:(c + 1) * n1, 0:LANES] = (pr * twr - pi * twi).astype(BF16)
        qbuf[c * n1:(c + 1) * n1, LANES:2 * LANES] = (pr * twi + pi * twr).astype(BF16)


def _hyspec_kernel(k_ref, sc_ref, frow_ref, twr_ref, twi_ref, f2_ref, o_ref, pbuf, qbuf, *, cb, n1):
    _hy_forward(k_ref[...], frow_ref, twr_ref, twi_ref, pbuf, qbuf, cb=cb, n1=n1, scale=sc_ref[...])
    o_ref[...] = jnp.dot(qbuf[...], f2_ref[...], preferred_element_type=F32).astype(o_ref.dtype)


def hyena_spectrum(kmat, scale, tabs, n1, cb):
    nch = kmat.shape[1] // LANES
    nr = _spec_rows(n1)
    full = lambda s: pl.BlockSpec(s, lambda i: (0, 0))
    return pl.pallas_call(
        functools.partial(_hyspec_kernel, cb=cb, n1=nr),
        grid=(nch // cb,),
        in_specs=[pl.BlockSpec((n1, cb * LANES), lambda i: (0, i)),
                  pl.BlockSpec((1, cb * LANES), lambda i: (0, i)),
                  full((2 * nr, n1)), full((nr, LANES)), full((nr, LANES)), full((2 * LANES, 2 * LANES))],
        out_specs=pl.BlockSpec((cb * nr, 2 * LANES), lambda i: (i, 0)),
        out_shape=jax.ShapeDtypeStruct((nch * nr, 2 * LANES), BF16),
        scratch_shapes=[pltpu.VMEM((2 * nr, cb * LANES), F32), pltpu.VMEM((cb * nr, 2 * LANES), BF16)],
        compiler_params=_cp("parallel"),
        name="hyena_spectrum",
    )(kmat, scale, tabs["frow_full"], tabs["twr"], tabs["twi"], tabs["f2"])


def _hyconv_kernel(z_ref, g_ref, ks_ref, bias_ref, frow_ref, twr_ref, twi_ref, f2_ref, f2i_ref, finv_ref,
                   o_ref, pbuf, qbuf, sbuf, *, cb, n1, sub):
    z = z_ref[...]
    _hy_forward(z.astype(BF16), frow_ref, twr_ref, twi_ref, pbuf, qbuf, cb=cb, n1=n1)
    twr, twi = twr_ref[...], twi_ref[...]
    rows = sub * n1
    for s in range(cb // sub):
        rs = slice(s * rows, (s + 1) * rows)
        zz = jnp.dot(qbuf[rs, :], f2_ref[...], preferred_element_type=F32)
        zr, zi = zz[:, 0:LANES], zz[:, LANES:]
        kr, ki = ks_ref[rs, 0:LANES].astype(F32), ks_ref[rs, LANES:2 * LANES].astype(F32)
        y = jnp.concatenate([zr * kr - zi * ki, zr * ki + zi * kr], axis=1).astype(BF16)
        rr = jnp.dot(y, f2i_ref[...], preferred_element_type=F32)
        for cc in range(sub):
            c = s * sub + cc
            r_re = rr[cc * n1:(cc + 1) * n1, 0:LANES]
            r_im = rr[cc * n1:(cc + 1) * n1, LANES:]
            sbuf[0:n1, c * LANES:(c + 1) * LANES] = (r_re * twr + r_im * twi).astype(BF16)
            sbuf[n1:2 * n1, c * LANES:(c + 1) * LANES] = (r_im * twr - r_re * twi).astype(BF16)
    y = jnp.dot(finv_ref[...], sbuf[...], preferred_element_type=F32)
    o_ref[...] = (y + z * bias_ref[...]) * g_ref[...]


def hyena_conv(z, gate, kspec, order, bias, tabs, n1, cb, sub):
    rows = z[0].shape[0]
    nch, width = HY_C, HY_C * LANES
    nr = _spec_rows(n1)
    koff = order * (nch // cb)
    full = lambda s: pl.BlockSpec(s, lambda i: (0, 0))
    io = pl.BlockSpec((rows, cb * LANES), lambda i: (0, i))
    sel = lambda a: pl.BlockSpec((rows, cb * LANES), lambda i: (0, i + a[1] // cb))
    return pl.pallas_call(
        functools.partial(_hyconv_kernel, cb=cb, n1=nr, sub=sub),
        grid=(nch // cb,),
        in_specs=[sel(z), sel(gate),
                  pl.BlockSpec((cb * nr, 2 * LANES), lambda i: (i + koff, 0)),
                  pl.BlockSpec((1, cb * LANES), lambda i: (0, i)),
                  full((2 * nr, rows)), full((nr, LANES)), full((nr, LANES)),
                  full((2 * LANES, 2 * LANES)), full((2 * LANES, 2 * LANES)), full((rows, 2 * nr))],
        out_specs=io,
        out_shape=jax.ShapeDtypeStruct((rows, width), F32),
        scratch_shapes=[pltpu.VMEM((2 * nr, cb * LANES), F32), pltpu.VMEM((cb * nr, 2 * LANES), BF16),
                        pltpu.VMEM((2 * nr, cb * LANES), BF16)],
        compiler_params=_cp("parallel"),
        name="hyena_conv",
    )(z[0], gate[0], kspec, bias, tabs["frow"], tabs["twr"], tabs["twi"], tabs["f2"], tabs["f2i"], tabs["finv"])


def _to_ab(x, rows):
    n, c = x.shape
    y = x.reshape(n // LANES, LANES, c).transpose(0, 2, 1).reshape(n // LANES, c * LANES)
    return jnp.pad(y, ((0, rows - n // LANES), (0, 0)))


def _from_ab(y, n):
    c = y.shape[1] // LANES
    return y[:n // LANES].reshape(n // LANES, c, LANES).transpose(0, 2, 1).reshape(n, c)


def hyena_mixer(xc, filt, bias):
    seq = xc.shape[0]
    if seq >= 1024:
        n1, rows, cb, sub = 2 * seq // LANES, seq // LANES, 32, 2
    else:
        n1, rows, cb, sub = 16, 16, 128, 16
    nfull = n1 * LANES
    tabs = _dft_tables(n1, rows, rows)
    tabs["frow_full"] = _dft_tables(n1, n1, rows)["frow"]
    kfull, asum = hyena_filter(seq, nfull, *filt)
    scale = jnp.repeat(1.0 / (asum[0] + EPS), LANES).reshape(1, -1)
    kspec = hyena_spectrum(_to_ab(kfull, n1), scale, tabs, n1, cb)
    xab = _to_ab(xc, rows)
    bias_l = jnp.repeat(bias.astype(F32), LANES, axis=1)
    z = hyena_conv((xab, 2 * HY_C), (xab, 0), kspec, 0, bias_l[0:1], tabs, n1, cb, sub)
    y = hyena_conv((z, 0), (xab, HY_C), kspec, 1, bias_l[1:2], tabs, n1, cb, sub)
    return _from_ab(y, seq)


def _dscan_kernel(*refs, rev, n_groups, upg, hpu, wd, use_kap):
    if use_kap:
        q_ref, k_ref, v_ref, la_ref, kap_ref, y_ref, st_ref = refs
    else:
        q_ref, k_ref, v_ref, la_ref, y_ref, st_ref = refs
    i = pl.program_id(0)

    @pl.when(i == 0)
    def _():
        st_ref[...] = jnp.zeros_like(st_ref)

    c = q_ref.shape[0]
    row = lax.broadcasted_iota(jnp.int32, (c, c), 0)
    col = lax.broadcasted_iota(jnp.int32, (c, c), 1)
    mask = (row <= col) if rev else (row >= col)
    tri = jnp.where(mask, 1.0, 0.0).astype(BF16)
    cum = _cumsum_rows(tri, la_ref[...])
    cum_t = cum.T
    c_end = cum[0:1, :] if rev else cum[c - 1:c, :]
    e_all = jnp.exp(cum)
    w_all = jnp.exp(c_end - cum)
    e_end = jnp.exp(c_end)
    if use_kap:
        kap = kap_ref[...]
        w_all = w_all * kap
    lane = lax.broadcasted_iota(jnp.int32, (c, wd), 1)
    lane1 = lax.broadcasted_iota(jnp.int32, (1, wd), 1)
    sub = wd // hpu

    def lanesel(cols, ln):
        out = cols[-1]
        for hh in range(hpu - 2, -1, -1):
            out = jnp.where(ln < (hh + 1) * sub, cols[hh], out)
        return out

    for g in range(n_groups):
        qg = q_ref[:, g * LANES:(g + 1) * LANES].astype(BF16)
        kg = k_ref[:, g * LANES:(g + 1) * LANES]
        gm = lax.dot_general(qg, kg.astype(BF16), (((1,), (1,)), ((), ())), preferred_element_type=F32)
        kg_t = kg.T.astype(BF16)
        for uu in range(upg):
            u = g * upg + uu
            v = v_ref[:, u * wd:(u + 1) * wd]
            heads = [u * hpu + hh for hh in range(hpu)]
            st = st_ref[u]
            y = lanesel([e_all[:, h:h + 1] for h in heads], lane) * jnp.dot(
                qg, st.astype(BF16), preferred_element_type=F32)
            for hh, h in enumerate(heads):
                dmat = jnp.exp(jnp.where(mask, cum[:, h:h + 1] - cum_t[h:h + 1, :], NEG))
                vh = v * kap[:, h:h + 1] if use_kap else v
                if hpu > 1:
                    vh = jnp.where((lane >= hh * sub) & (lane < (hh + 1) * sub), vh, 0.0)
                y = y + jnp.dot((gm * dmat).astype(BF16), vh.astype(BF16), preferred_element_type=F32)
            y_ref[:, u * wd:(u + 1) * wd] = y
            vw = v * lanesel([w_all[:, h:h + 1] for h in heads], lane)
            st_ref[u] = (lanesel([e_end[:, h:h + 1] for h in heads], lane1) * st
                         + jnp.dot(kg_t, vw.astype(BF16), preferred_element_type=F32))


def _chunk_order(n, n_lat, c, rev):
    nl, nt = n_lat // c, n // c
    if rev:
        return lambda i: nt - 1 - i
    return lambda i: jnp.where(i < nt - nl, nl + i, i - (nt - nl))


def decay_scan(q, k, v, la, kap, n_lat, *, rev, n_groups, upg, hpu, wd):
    n = q[0].shape[0]
    c = SCAN_CHUNK
    n_units = n_groups * upg
    rmap = _chunk_order(n, n_lat, c, rev)

    def spec(width, blk):
        return pl.BlockSpec((c, width), lambda i: (rmap(i), blk))

    ins = [q, k, v, la] + ([kap] if kap is not None else [])
    widths = [n_groups * LANES, n_groups * LANES, n_units * wd, LANES] + ([LANES] if kap is not None else [])
    return pl.pallas_call(
        functools.partial(_dscan_kernel, rev=rev, n_groups=n_groups, upg=upg, hpu=hpu, wd=wd,
                          use_kap=kap is not None),
        grid=(n // c,),
        in_specs=[spec(w, a[1]) for w, a in zip(widths, ins)],
        out_specs=pl.BlockSpec((c, n_units * wd), lambda i: (rmap(i), 0)),
        out_shape=jax.ShapeDtypeStruct((n, n_units * wd), F32),
        scratch_shapes=[pltpu.VMEM((n_units, LANES, wd), F32)],
        compiler_params=_cp("arbitrary"),
        name="decay_scan_rev" if rev else "decay_scan_fwd",
    )(*[a[0] for a in ins])


def _post_kernel(*refs, n_terms, skip, pre_gate, ngroups):
    terms = refs[:n_terms]
    pos = n_terms
    y = terms[0][...]
    for t in terms[1:]:
        y = y + t[...]
    if skip:
        y = y + refs[pos][...] * refs[pos + 1][...]
        pos += 2
    gate_ref, ng_ref, o_ref = refs[pos:pos + 3]
    gate = _silu(gate_ref[...])
    if pre_gate:
        y = y * gate
    gw = y.shape[1] // ngroups
    for g in range(ngroups):
        sl = slice(g * gw, (g + 1) * gw)
        yg = y[:, sl]
        o = yg * lax.rsqrt(jnp.mean(yg * yg, axis=-1, keepdims=True) + EPS) * ng_ref[:, sl]
        if not pre_gate:
            o = o * gate[:, sl]
        o_ref[:, sl] = o.astype(o_ref.dtype)


def mixer_post(terms, skip, gate, norm_g, *, pre_gate, ngroups, width=1024, tm=256):
    n = terms[0][0].shape[0]
    tm = min(tm, n)
    blk = lambda a: pl.BlockSpec((tm, width), lambda i: (i, a[1]))
    vec = pl.BlockSpec((1, width), lambda i: (0, 0))
    arrays = [t[0] for t in terms]
    specs = [blk(t) for t in terms]
    if skip is not None:
        arrays += [skip[0][0], skip[1].reshape(1, width)]
        specs += [blk(skip[0]), vec]
    arrays += [gate[0], norm_g.reshape(1, width)]
    specs += [blk(gate), vec]
    return pl.pallas_call(
        functools.partial(_post_kernel, n_terms=len(terms), skip=skip is not None, pre_gate=pre_gate,
                          ngroups=ngroups),
        grid=(n // tm,),
        in_specs=specs,
        out_specs=pl.BlockSpec((tm, width), lambda i: (i, 0)),
        out_shape=jax.ShapeDtypeStruct((n, width), BF16),
        compiler_params=_cp("parallel"),
        name="mixer_post",
    )(*arrays)


def _ssd_dt_kernel(x_ref, b_ref, a_ref, dt_ref, la_ref):
    x = x_ref[...] + b_ref[...]
    dt = jnp.maximum(x, 0.0) + jnp.log1p(jnp.exp(-jnp.abs(x)))
    dt_ref[...] = dt
    la_ref[...] = dt * a_ref[...]


def ssd_mixer(u, dt_raw, conv_w, conv_b, dt_bias, a_log, d_skip, norm_g, n_lat):
    n = u.shape[0]
    xbc = conv3(u, conv_w, conv_b, True, starts=(0, n_lat), col0=SSD_INNER)
    pad = lambda t: jnp.pad(t.astype(F32), ((0, 0), (0, LANES - SSD_HEADS))).reshape(1, 2 * LANES)
    tm = 256
    dt, la = pl.pallas_call(
        _ssd_dt_kernel,
        grid=(n // tm,),
        in_specs=[pl.BlockSpec((tm, 2 * LANES), lambda i: (i, 0)),
                  pl.BlockSpec((1, 2 * LANES), lambda i: (0, 0)),
                  pl.BlockSpec((1, 2 * LANES), lambda i: (0, 0))],
        out_specs=[pl.BlockSpec((tm, 2 * LANES), lambda i: (i, 0))] * 2,
        out_shape=[jax.ShapeDtypeStruct((n, 2 * LANES), F32)] * 2,
        compiler_params=_cp("parallel"),
        name="ssd_dt",
    )(dt_raw, pad(dt_bias), pad(-jnp.exp(a_log.astype(F32))))
    ys = [(decay_scan((xbc, 3), (xbc, 2), (xbc, 0), (la, d), (dt, d), n_lat,
                      rev=bool(d), n_groups=SSD_GROUPS, upg=2, hpu=2, wd=LANES), 0) for d in range(2)]
    dsk = jnp.repeat(d_skip.astype(F32), SSD_P)
    return mixer_post(ys, ((xbc, 0), dsk), (u, 0), norm_g, pre_gate=True, ngroups=SSD_GROUPS)


def _rope_kernel(q_ref, k_ref, inv_ref, qo_ref, ko_ref, *, n_lat, n_ctx):
    tm = q_ref.shape[0]
    row = pl.program_id(0) * tm + lax.broadcasted_iota(jnp.int32, (tm, 1), 0)
    pos = jnp.where(row < n_lat, row + n_ctx, row - n_lat).astype(F32)
    ang = pos * inv_ref[...]
    cs, sn = jnp.cos(ang), jnp.sin(ang)
    lane = lax.broadcasted_iota(jnp.int32, (tm, LANES), 1)
    sgn = jnp.where(lane < RET_DK // 2, -sn, sn)
    for h in range(RET_HEADS):
        sl = slice(h * RET_DK, (h + 1) * RET_DK)
        for src, dst, scale in ((q_ref, qo_ref, 1.0), (k_ref, ko_ref, RET_DK ** -0.5)):
            x = src[:, sl]
            y = x * cs + pltpu.roll(x, RET_DK // 2, 1) * sgn
            dst[:, sl] = y * scale


def retention_mixer(u, norm_g, n_lat):
    n = u.shape[0]
    tm = 256
    half = RET_DK // 2
    inv = ROPE_BASE ** (-jnp.arange(half, dtype=F32) / half)
    inv2 = jnp.concatenate([inv, inv]).reshape(1, LANES)
    wq = RET_HEADS * RET_DK
    qr, kr = pl.pallas_call(
        functools.partial(_rope_kernel, n_lat=n_lat, n_ctx=n - n_lat),
        grid=(n // tm,),
        in_specs=[pl.BlockSpec((tm, wq), lambda i: (i, 0)), pl.BlockSpec((tm, wq), lambda i: (i, 1)),
                  pl.BlockSpec((1, LANES), lambda i: (0, 0))],
        out_specs=[pl.BlockSpec((tm, wq), lambda i: (i, 0))] * 2,
        out_shape=[jax.ShapeDtypeStruct((n, wq), F32)] * 2,
        compiler_params=_cp("parallel"),
        name="rope",
    )(u, u, inv2)
    log_gamma = jnp.log1p(-jnp.exp2(-5.0 - jnp.arange(RET_HEADS, dtype=F32)))
    la = jnp.broadcast_to(jnp.pad(log_gamma, (0, LANES - RET_HEADS)).reshape(1, LANES), (n, LANES))
    ys = [(decay_scan((qr, 0), (kr, 0), (u, 1), (la, 0), None, n_lat,
                      rev=bool(d), n_groups=RET_HEADS, upg=1, hpu=1, wd=RET_DV), 0) for d in range(2)]
    return mixer_post(ys, None, (u, 2), norm_g, pre_gate=False, ngroups=RET_HEADS)


def _anchor_rows(gc, b, rev):
    c, n = gc.shape
    shift = b if rev else b - 1
    if 2 * b >= 8:
        return jnp.concatenate([jnp.broadcast_to(gc[s + shift:s + shift + 1, :], (2 * b, n))
                                for s in range(0, c, 2 * b)], axis=0)
    sub = lax.broadcasted_iota(jnp.int32, (8, n), 0)
    tiles = []
    for t in range(0, c, 8):
        tile = None
        for m in range(8 // (2 * b)):
            r = t + m * 2 * b + shift
            cand = jnp.broadcast_to(gc[r:r + 1, :], (8, n))
            tile = cand if tile is None else jnp.where(sub >= m * 2 * b, cand, tile)
        tiles.append(tile)
    return jnp.concatenate(tiles, axis=0)


def _hg_kernel(q_ref, f_ref, v_ref, lb_ref, y_ref, st_ref, *, rev):
    i = pl.program_id(0)

    @pl.when(i == 0)
    def _():
        st_ref[...] = jnp.zeros_like(st_ref)

    c, width = q_ref.shape
    nh = width // HG_DK
    lb = lb_ref[...]
    qh = _silu(q_ref[...])
    fr = f_ref[...]
    lg = jnp.log(lb + (1.0 - lb) * jax.nn.sigmoid(fr))
    kk = (1.0 - lb) * jax.nn.sigmoid(-fr)
    row = lax.broadcasted_iota(jnp.int32, (c, c), 0)
    col = lax.broadcasted_iota(jnp.int32, (c, c), 1)
    tri = jnp.where((row <= col) if rev else (row >= col), 1.0, 0.0).astype(BF16)
    gc = _cumsum_rows(tri, lg)
    g_end = gc[0:1, :] if rev else gc[c - 1:c, :]
    qe = (qh * jnp.exp(gc)).astype(BF16)
    kw = (kk * jnp.exp(g_end - gc)).astype(BF16)
    dec = jnp.exp(g_end)
    vb = v_ref[...].astype(BF16)
    qb, kb = qh.astype(BF16), kk.astype(BF16)
    nt = (((1,), (1,)), ((), ()))
    hs = [slice(h * HG_DK, (h + 1) * HG_DK) for h in range(nh)]
    scores = [jnp.where(row == col, lax.dot_general(qb[:, s], kb[:, s], nt, preferred_element_type=F32), 0.0)
              for s in hs]
    ridx = lax.broadcasted_iota(jnp.int32, (c, 1), 0)
    lvl = 0
    while (1 << lvl) < c:
        b = 1 << lvl
        anchor = _anchor_rows(gc, b, rev)
        upper = ((ridx >> lvl) & 1) == 1
        q_on, k_on = (~upper, upper) if rev else (upper, ~upper)
        qt = (qh * jnp.exp(jnp.where(q_on, gc - anchor, NEG))).astype(BF16)
        kt = (kk * jnp.exp(jnp.where(k_on, anchor - gc, NEG))).astype(BF16)
        same = (row >> (lvl + 1)) == (col >> (lvl + 1))
        for h, s in enumerate(hs):
            sc = lax.dot_general(qt[:, s], kt[:, s], nt, preferred_element_type=F32)
            scores[h] = scores[h] + jnp.where(same, sc, 0.0)
        lvl += 1
    for h, s in enumerate(hs):
        st = st_ref[h]
        y = jnp.dot(scores[h].astype(BF16), vb[:, s], preferred_element_type=F32)
        y = y + lax.dot_general(qe[:, s], st.astype(BF16), nt, preferred_element_type=F32)
        y_ref[:, s] = y
        st_ref[h] = dec[:, s] * st + lax.dot_general(vb[:, s], kw[:, s], (((0,), (0,)), ((), ())),
                                                     preferred_element_type=F32)


def hgrn2_scan(u, lb, n_lat, *, rev, d):
    n = u.shape[0]
    c = HG_CHUNK
    rmap = _chunk_order(n, n_lat, c, rev)
    spec = lambda blk: pl.BlockSpec((c, HG_W), lambda i: (rmap(i), blk))
    return pl.pallas_call(
        functools.partial(_hg_kernel, rev=rev),
        grid=(n // c,),
        in_specs=[spec(0), spec(1 + d), spec(3), pl.BlockSpec((1, HG_W), lambda i: (0, 0))],
        out_specs=pl.BlockSpec((c, HG_W), lambda i: (rmap(i), 0)),
        out_shape=jax.ShapeDtypeStruct((n, HG_W), F32),
        scratch_shapes=[pltpu.VMEM((HG_HEADS, HG_DK, HG_DK), F32)],
        compiler_params=_cp("arbitrary"),
        name="hgrn2_scan_rev" if rev else "hgrn2_scan_fwd",
    )(u, u, u, lb)


def hgrn2_mixer(u, lb, norm_g, n_lat):
    ys = [(hgrn2_scan(u, lb[d:d + 1].astype(F32), n_lat, rev=bool(d), d=d), 0) for d in range(2)]
    return mixer_post(ys, None, (u, 4), norm_g, pre_gate=False, ngroups=HG_HEADS)


_HIGH16 = 0xFFFF0000


def _pack_rows(y, o_ref):
    t, d = y.shape
    nj = d // (2 * LANES)
    words = []
    for j in range(nj):
        lo = y[:, 2 * j * LANES:(2 * j + 1) * LANES].astype(BF16).astype(F32)
        hi = y[:, (2 * j + 1) * LANES:(2 * j + 2) * LANES].astype(BF16).astype(F32)
        words.append((pltpu.bitcast(lo, jnp.uint32) >> 16) | (pltpu.bitcast(hi, jnp.uint32) & jnp.uint32(_HIGH16)))
    o_ref[...] = jnp.swapaxes(jnp.stack(words, axis=0), 0, 1).reshape(t * nj, LANES)


def _unpack_rows(p_ref, t, nj):
    return jnp.swapaxes(p_ref[...].reshape(t, nj, LANES), 0, 1)


def _unpack_piece(words, j):
    lo = pltpu.bitcast(words[j] << 16, F32)
    hi = pltpu.bitcast(words[j] & jnp.uint32(_HIGH16), F32)
    return lo, hi


def _router_kernel(x_ref, g_ref, sc_ref, sh_ref, wr_ref, br_ref, h_ref, r_ref, *, n_lat):
    x = x_ref[...]
    tm = x.shape[0]
    y = x * lax.rsqrt(jnp.mean(x * x, axis=-1, keepdims=True) + EPS) * g_ref[...]
    h = y * (1.0 + _row_select(sc_ref, tm, n_lat)) + _row_select(sh_ref, tm, n_lat)
    _pack_rows(h, h_ref)
    lg = jnp.dot(h, wr_ref[...], precision=HI, preferred_element_type=F32) + br_ref[...]
    lane = lax.broadcasted_iota(jnp.int32, lg.shape, 1).astype(F32)
    big = float(LANES)
    is_g = lane < MOE_GROUPS
    gmax = jnp.max(jnp.where(is_g, lg, -jnp.inf), axis=1, keepdims=True)
    gsel = jnp.min(jnp.where(is_g & (lg == gmax), lane, big), axis=1, keepdims=True)
    pg = 1.0 / jnp.sum(jnp.where(is_g, jnp.exp(lg - gmax), 0.0), axis=1, keepdims=True)
    lo = MOE_GROUPS + MOE_PER_GROUP * gsel
    in_e = (lane >= lo) & (lane < lo + MOE_PER_GROUP)
    v1 = jnp.max(jnp.where(in_e, lg, -jnp.inf), axis=1, keepdims=True)
    i1 = jnp.min(jnp.where(in_e & (lg == v1), lane, big), axis=1, keepdims=True)
    rest = in_e & (lane != i1)
    v2 = jnp.max(jnp.where(rest, lg, -jnp.inf), axis=1, keepdims=True)
    i2 = jnp.min(jnp.where(rest & (lg == v2), lane, big), axis=1, keepdims=True)
    t = jnp.exp(v2 - v1)
    w1 = pg / (1.0 + t)
    w2 = pg * t / (1.0 + t)
    r_ref[...] = jnp.where(lane == 0, i1 - MOE_GROUPS,
                           jnp.where(lane == 1, i2 - MOE_GROUPS,
                                     jnp.where(lane == 2, w1, jnp.where(lane == 3, w2, 0.0))))


def moe_router(x, g, sc, sh, wr, br, m, n_lat, tm=256):
    d = x.shape[1]
    nj = d // (2 * LANES)
    vec = pl.BlockSpec((1, d), lambda i: (0, 0))
    two = pl.BlockSpec((2, d), lambda i: (0, 0))
    return pl.pallas_call(
        functools.partial(_router_kernel, n_lat=n_lat),
        grid=(m // tm,),
        in_specs=[pl.BlockSpec((tm, d), lambda i: (i, 0)), vec, two, two,
                  pl.BlockSpec((d, LANES), lambda i: (0, 0)), pl.BlockSpec((1, LANES), lambda i: (0, 0))],
        out_specs=[pl.BlockSpec((tm * nj, LANES), lambda i: (i, 0)), pl.BlockSpec((tm, LANES), lambda i: (i, 0))],
        out_shape=[jax.ShapeDtypeStruct((m * nj, LANES), jnp.uint32), jax.ShapeDtypeStruct((m, LANES), F32)],
        compiler_params=_cp("parallel"),
        name="moe_router",
    )(x, g.reshape(1, d), sc, sh, wr, br)


def _moe_plan(route, m, n_tiles):
    t = MOE_TILE
    pe = jnp.concatenate([route[:, 0], route[:, 1]]).astype(jnp.int32)
    onehot = (pe[:, None] == jnp.arange(MOE_EXPERTS, dtype=jnp.int32)[None, :]).astype(jnp.int32)
    csum = jnp.cumsum(onehot, axis=0)
    rank = jnp.take_along_axis(csum, pe[:, None], axis=1)[:, 0] - 1
    cnt = csum[-1]
    ntile = (cnt + t - 1) // t
    tend = jnp.cumsum(ntile)
    tstart = tend - ntile
    slot = tstart[pe] * t + rank
    tid = jnp.arange(n_tiles, dtype=jnp.int32)
    tile_e = jnp.clip(jnp.searchsorted(tend, tid, side="right"), 0, MOE_EXPERTS - 1).astype(jnp.int32)
    tile_valid = jnp.where(tid < tend[-1], jnp.clip(cnt[tile_e] - (tid - tstart[tile_e]) * t, 0, t), 0)
    tok = jnp.tile(jnp.arange(m, dtype=jnp.int32), 2)
    row_tok = jnp.zeros((n_tiles * t,), jnp.int32).at[slot].set(tok)
    return row_tok, tile_e, tile_valid.astype(jnp.int32), slot[:m], slot[m:]


def _row_copies(idx_ref, base, src_hbm, dst, sem, rows, nj, wait):
    if wait:
        pltpu.make_async_copy(src_hbm.at[pl.ds(0, rows * nj), :], dst, sem).wait()
        return

    def body(r, carry):
        src = idx_ref[base + r] * nj
        pltpu.make_async_copy(src_hbm.at[pl.ds(src, nj), :], dst.at[pl.ds(r * nj, nj), :], sem).start()
        return carry
    lax.fori_loop(0, rows, body, 0, unroll=8)


def _expert_kernel(te_ref, tv_ref, tok_ref, hp_hbm, w1_ref, w3_ref, w2_ref, o_ref, buf, xbuf, sem, *, nj):
    i = pl.program_id(0)
    n = pl.num_programs(0)
    t = xbuf.shape[0]

    def fetch(tile, slot):
        _row_copies(tok_ref, tile * t, hp_hbm, buf.at[slot], sem.at[slot], t, nj, wait=False)

    @pl.when((i == 0) & (tv_ref[0] > 0))
    def _():
        fetch(0, 0)

    @pl.when((i + 1 < n) & (tv_ref[jnp.minimum(i + 1, n - 1)] > 0))
    def _():
        fetch(i + 1, (i + 1) % 2)

    @pl.when(tv_ref[i] > 0)
    def _():
        slot = i % 2
        cur = buf.at[slot]
        _row_copies(tok_ref, 0, hp_hbm, cur, sem.at[slot], t, nj, wait=True)
        words = _unpack_rows(cur, t, nj)
        for j in range(nj):
            lo, hi = _unpack_piece(words, j)
            xbuf[:, 2 * j * LANES:(2 * j + 1) * LANES] = lo.astype(BF16)
            xbuf[:, (2 * j + 1) * LANES:(2 * j + 2) * LANES] = hi.astype(BF16)
        x = xbuf[...]
        a = jnp.dot(x, w1_ref[0, 0], preferred_element_type=F32)
        b = jnp.dot(x, w3_ref[0, 0], preferred_element_type=F32)
        y = jnp.dot((_silu(a) * b).astype(BF16), w2_ref[0, 0], preferred_element_type=F32)
        _pack_rows(y, o_ref)

    @pl.when(tv_ref[i] == 0)
    def _():
        o_ref[...] = jnp.zeros_like(o_ref)


def moe_experts(hp, row_tok, tile_e, tile_valid, w1, w3, w2, layer):
    t = MOE_TILE
    n_tiles = tile_e.shape[0]
    d, ff = w1.shape[2], w1.shape[3]
    nj = d // (2 * LANES)
    wmap = lambda i, te, tv, tok: (layer, te[i], 0, 0)
    return pl.pallas_call(
        functools.partial(_expert_kernel, nj=nj),
        grid_spec=pltpu.PrefetchScalarGridSpec(
            num_scalar_prefetch=3, grid=(n_tiles,),
            in_specs=[pl.BlockSpec(memory_space=pl.ANY),
                      pl.BlockSpec((1, 1, d, ff), wmap), pl.BlockSpec((1, 1, d, ff), wmap),
                      pl.BlockSpec((1, 1, ff, d), wmap)],
            out_specs=pl.BlockSpec((t * nj, LANES), lambda i, te, tv, tok: (i, 0)),
            scratch_shapes=[pltpu.VMEM((2, t * nj, LANES), jnp.uint32), pltpu.VMEM((t, d), BF16),
                            pltpu.SemaphoreType.DMA((2,))]),
        out_shape=jax.ShapeDtypeStruct((n_tiles * t * nj, LANES), jnp.uint32),
        compiler_params=_cp("arbitrary"),
        name="moe_experts",
    )(tile_e, tile_valid, row_tok, hp, w1, w3, w2)


def _combine_kernel(s0_ref, s1_ref, x_ref, r_ref, g_ref, fg_ref, ys_hbm, o_ref, buf, sem, *, n_lat, final, nj):
    tt = x_ref.shape[0]
    base = pl.program_id(0) * tt
    for k, s_ref in enumerate((s0_ref, s1_ref)):
        _row_copies(s_ref, base, ys_hbm, buf.at[k], sem.at[k], tt, nj, wait=False)
    for k, s_ref in enumerate((s0_ref, s1_ref)):
        _row_copies(s_ref, base, ys_hbm, buf.at[k], sem.at[k], tt, nj, wait=True)
    r = r_ref[...]
    w0 = jnp.broadcast_to(r[:, 2:3], (tt, LANES))
    w1 = jnp.broadcast_to(r[:, 3:4], (tt, LANES))
    is_ctx = base >= n_lat
    words0 = _unpack_rows(buf.at[0], tt, nj)
    words1 = _unpack_rows(buf.at[1], tt, nj)
    for j in range(nj):
        p0 = _unpack_piece(words0, j)
        p1 = _unpack_piece(words1, j)
        for half in range(2):
            cs = slice((2 * j + half) * LANES, (2 * j + half + 1) * LANES)
            gate = jnp.where(is_ctx, g_ref[1:2, cs], g_ref[0:1, cs])
            o_ref[:, cs] = x_ref[:, cs] + gate * (w0 * p0[half] + w1 * p1[half])
    if final:
        y = o_ref[...]
        o_ref[...] = y * lax.rsqrt(jnp.mean(y * y, axis=-1, keepdims=True) + EPS) * fg_ref[...]


def moe_combine(x, route, gate, final_g, ys, slot0, slot1, m, n_lat, final, tt=256):
    d = x.shape[1]
    nj = d // (2 * LANES)
    assert n_lat % tt == 0
    return pl.pallas_call(
        functools.partial(_combine_kernel, n_lat=n_lat, final=final, nj=nj),
        grid_spec=pltpu.PrefetchScalarGridSpec(
            num_scalar_prefetch=2, grid=(m // tt,),
            in_specs=[pl.BlockSpec((tt, d), lambda i, a, b: (i, 0)),
                      pl.BlockSpec((tt, LANES), lambda i, a, b: (i, 0)),
                      pl.BlockSpec((2, d), lambda i, a, b: (0, 0)),
                      pl.BlockSpec((1, d), lambda i, a, b: (0, 0)),
                      pl.BlockSpec(memory_space=pl.ANY)],
            out_specs=pl.BlockSpec((tt, d), lambda i, a, b: (i, 0)),
            scratch_shapes=[pltpu.VMEM((2, tt * nj, LANES), jnp.uint32), pltpu.SemaphoreType.DMA((2,))]),
        out_shape=jax.ShapeDtypeStruct((m, d), F32),
        compiler_params=_cp("arbitrary"),
        name="moe_combine",
    )(slot0, slot1, x, route, gate, final_g.reshape(1, d), ys)


def hier_moe(x, norm_g, sc, sh, gate, P, final_g, m, n_lat, final):
    n_tiles = 2 * m // MOE_TILE + MOE_EXPERTS
    hp, route = moe_router(x, norm_g, sc, sh, P["moe_wr"], P["moe_br"], m, n_lat)
    row_tok, tile_e, tile_valid, slot0, slot1 = _moe_plan(route, m, n_tiles)
    ys = moe_experts(hp, row_tok, tile_e, tile_valid, P["moe_w1"], P["moe_w3"], P["moe_w2"], P["layer"])
    return moe_combine(x, route, gate, final_g, ys, slot0, slot1, m, n_lat, final)


def _to_cols(t, rows):
    n, ch = t.shape
    return jnp.swapaxes(t.reshape(rows, GRID_W, ch), 0, 1).reshape(n, ch)


def _from_cols(t, rows):
    n, ch = t.shape
    return jnp.swapaxes(t.reshape(GRID_W, rows, ch), 0, 1).reshape(n, ch)


def _layer(xa, mod, P, lb, n_lat, last, final_g):
    n, d = xa.shape
    sh1, sc1, g1, sh2, sc2, g2 = (mod[:, q * d:(q + 1) * d] for q in range(6))
    grid_rows = n_lat // GRID_W
    h, h_lat = norm_modulate(xa, P["norm1"], sc1, sh1, n_lat)
    h_cols = jnp.concatenate([_to_cols(h_lat, grid_rows), h[n_lat:]], axis=0)
    tm = 768 if n % 768 == 0 else 256
    layer, w_in = P["layer"], P["w_in"]
    u_hy = matmul(h, w_in["hy"], layer, 0, 3072, tm)
    u_ssd = matmul(h, w_in["ssd"], layer, 0, 3072, tm)
    u_hg = matmul(h_cols, w_in["hg"], layer, 0, 5120, tm)
    u_ret = matmul(h, w_in["ret"], layer, 0, 3072, tm)
    dt_raw = matmul(h, P["w_dt"], layer, 0, 2 * LANES, tm, tn=2 * LANES)
    filt = (P["hy_w1"], P["hy_b1"], P["hy_fr1"], P["hy_w2"], P["hy_b2"], P["hy_fr2"], P["hy_w3"])
    xc_lat, xc_ctx = conv3(u_hy, P["hy_conv_w"], P["hy_conv_b"], False, starts=(0, n_lat), split=True)
    a = hyena_mixer(xc_lat, filt, P["hy_bias"]).astype(BF16)
    if not last:
        a = jnp.concatenate([a, hyena_mixer(xc_ctx, filt, P["hy_bias"]).astype(BF16)], axis=0)
    b = ssd_mixer(u_ssd, dt_raw, P["ssd_conv_w"], P["ssd_conv_b"], P["ssd_dt_bias"], P["ssd_a_log"],
                  P["ssd_d"], P["ssd_norm_g"], n_lat)
    c = hgrn2_mixer(u_hg, lb, P["hg_norm_g"], n_lat)
    c = jnp.concatenate([_from_cols(c[:n_lat], grid_rows), c[n_lat:]], axis=0)
    dd = retention_mixer(u_ret, P["ret_norm_g"], n_lat)
    m = n_lat if last else n
    xa = matmul_out((a, b, c, dd), P["w_out"], layer, xa, g1, m, n_lat, tm=1024 if m % 1024 == 0 else tm)
    return hier_moe(xa, P["norm2"], sc2, sh2, g2, P, final_g, m, n_lat, last)


def kernel(x, c, ctx, c_ctx, ada_w, ada_b, norm1_g, norm2_g, w_in, w_out, hy_conv_w, hy_conv_b, hy_w1, hy_b1, hy_fr1, hy_w2, hy_b2, hy_fr2, hy_w3, hy_bias, ssd_conv_w, ssd_conv_b, ssd_dt_bias, ssd_a_log, ssd_d, ssd_norm_g, hg_lb_raw, hg_norm_g, ret_norm_g, moe_wg, moe_bg, moe_we, moe_be, moe_w1, moe_w3, moe_w2, final_g):
    depth = ada_w.shape[0]
    n_lat = x.shape[1]
    lb_prob = jax.nn.softmax(hg_lb_raw.astype(F32), axis=0)
    lb_all = jnp.cumsum(lb_prob, axis=0) - lb_prob[0]
    mods = ada_modulation(c, c_ctx, ada_w, ada_b)
    xa = jnp.concatenate([x[0], ctx[0]], axis=0)
    o_ssd, o_dt, o_hg, o_ret = 3072, 6144, 6176, 11296
    w_groups = {"hy": w_in[:, :, :o_ssd].astype(BF16), "ssd": w_in[:, :, o_ssd:o_dt].astype(BF16),
                "hg": w_in[:, :, o_hg:o_ret].astype(BF16), "ret": w_in[:, :, o_ret:].astype(BF16)}
    zpad = jnp.zeros(w_in.shape[:2] + (LANES - SSD_HEADS,), w_in.dtype)
    w_dt_b = jnp.concatenate([w_in[:, :, o_dt:o_dt + SSD_HEADS], zpad,
                              w_in[:, :, o_dt + SSD_HEADS:o_hg], zpad], axis=2).astype(BF16)
    w_out_b = w_out.astype(BF16)
    moe_b = [w.astype(BF16) for w in (moe_w1, moe_w3, moe_w2)]
    for l in range(depth):
        pad_r = LANES - MOE_GROUPS - MOE_EXPERTS
        P = {
            "layer": l, "norm1": norm1_g[l], "norm2": norm2_g[l], "w_in": w_groups, "w_dt": w_dt_b, "w_out": w_out_b,
            "hy_conv_w": hy_conv_w[l], "hy_conv_b": hy_conv_b[l], "hy_w1": hy_w1[l], "hy_b1": hy_b1[l],
            "hy_fr1": hy_fr1[l], "hy_w2": hy_w2[l], "hy_b2": hy_b2[l], "hy_fr2": hy_fr2[l], "hy_w3": hy_w3[l],
            "hy_bias": hy_bias[l],
            "ssd_conv_w": ssd_conv_w[l], "ssd_conv_b": ssd_conv_b[l], "ssd_dt_bias": ssd_dt_bias[l],
            "ssd_a_log": ssd_a_log[l], "ssd_d": ssd_d[l], "ssd_norm_g": ssd_norm_g[l],
            "hg_norm_g": hg_norm_g[l], "ret_norm_g": ret_norm_g[l],
            "moe_wr": jnp.pad(jnp.concatenate([moe_wg[l], moe_we[l]], axis=1).astype(F32), ((0, 0), (0, pad_r))),
            "moe_br": jnp.pad(jnp.concatenate([moe_bg[l], moe_be[l]]).astype(F32), (0, pad_r)).reshape(1, LANES),
            "moe_w1": moe_b[0], "moe_w3": moe_b[1], "moe_w2": moe_b[2],
        }
        xa = _layer(xa, mods[l, 0:2], P, lb_all[l], n_lat, l == depth - 1, final_g)
    return xa[None]
```

```python
import functools
import math

import numpy as np
import jax
import jax.numpy as jnp
from jax import lax
from jax.experimental import pallas as pl
from jax.experimental.pallas import tpu as pltpu

F32 = jnp.float32
BF16 = jnp.bfloat16
EPS = 1e-6
NEG = -1e30
LANES = 128
VMEM_LIMIT = 56 * 1024 * 1024

D_MODEL = 4096
GRID_W = 64
HY_C = 1024
HY_BANDS = 16
HY_FFN = 64
SSD_HEADS = 16
SSD_P = 64
SSD_INNER = SSD_HEADS * SSD_P
SSD_GROUPS = 4
SSD_N = 128
HG_HEADS = 8
HG_DK = 128
HG_W = HG_HEADS * HG_DK
RET_HEADS = 4
RET_DK = 128
RET_DV = 256
RET_W = RET_HEADS * RET_DV
ROPE_BASE = 10000.0
MOE_GROUPS = 4
MOE_PER_GROUP = 8
MOE_EXPERTS = 32
MOE_FF = 512
MOE_TILE = 256
SCAN_CHUNK = 128
SCAN_STEP_CHUNKS = 2
HG_CHUNK = 128
HI = lax.Precision.HIGHEST


def _cp(*sem):
    return pltpu.CompilerParams(dimension_semantics=sem, vmem_limit_bytes=VMEM_LIMIT)


def _silu(x):
    return x * jax.nn.sigmoid(x)


def _split3(x):
    hi = x.astype(BF16)
    r1 = x - hi.astype(F32)
    mid = r1.astype(BF16)
    lo = (r1 - mid.astype(F32)).astype(BF16)
    return hi, mid, lo


def _cumsum_rows(tri, x):
    hi, mid, lo = _split3(x)
    return (jnp.dot(tri, hi, preferred_element_type=F32)
            + jnp.dot(tri, mid, preferred_element_type=F32)
            + jnp.dot(tri, lo, preferred_element_type=F32))


def _ada_kernel(cb_ref, w_ref, b_ref, o_ref, acc_ref, *, nk):
    k = pl.program_id(2)

    @pl.when(k == 0)
    def _():
        acc_ref[...] = jnp.zeros_like(acc_ref)

    tk, tn = w_ref.shape[1], w_ref.shape[2]
    for s in range(2):
        sv = _silu(cb_ref[s])
        for j in range(tn // LANES):
            p = w_ref[0, :, j * LANES:(j + 1) * LANES] * sv
            acc_ref[s, :, j * LANES:(j + 1) * LANES] += p.reshape(tk // 8, 8, LANES).sum(axis=0)

    @pl.when(k == nk - 1)
    def _():
        r0 = acc_ref[0].sum(axis=0, keepdims=True) + b_ref[0]
        r1 = acc_ref[1].sum(axis=0, keepdims=True) + b_ref[0]
        row = lax.broadcasted_iota(jnp.int32, (8, tn), 0)
        o_ref[0] = jnp.where(row == 0, r0, jnp.where(row == 1, r1, 0.0))


def ada_modulation(c, c_ctx, ada_w, ada_b):
    depth, d, n = ada_w.shape
    tk, tn = min(512, d), math.gcd(n, 2048)
    cb = jnp.stack([jnp.broadcast_to(c.reshape(d, 1), (d, LANES)),
                    jnp.broadcast_to(c_ctx.reshape(d, 1), (d, LANES))])
    nk = d // tk
    return pl.pallas_call(
        functools.partial(_ada_kernel, nk=nk),
        grid=(depth, n // tn, nk),
        in_specs=[pl.BlockSpec((2, tk, LANES), lambda l, j, k: (0, k, 0)),
                  pl.BlockSpec((1, tk, tn), lambda l, j, k: (l, k, j)),
                  pl.BlockSpec((1, 1, tn), lambda l, j, k: (l, 0, j))],
        out_specs=pl.BlockSpec((1, 8, tn), lambda l, j, k: (l, 0, j)),
        out_shape=jax.ShapeDtypeStruct((depth, 8, n), F32),
        scratch_shapes=[pltpu.VMEM((2, 8, tn), F32)],
        compiler_params=_cp("arbitrary", "arbitrary", "arbitrary"),
        name="ada_modulation",
    )(cb, ada_w, ada_b.reshape(depth, 1, n))


def _row_select(ref, tm, n_lat):
    row = pl.program_id(0) * tm + lax.broadcasted_iota(jnp.int32, (tm, 1), 0)
    return jnp.where(row < n_lat, ref[0:1, :], ref[1:2, :])


def _norm_kernel(x_ref, g_ref, sc_ref, sh_ref, o_ref, lat_ref, *, n_lat):
    x = x_ref[...]
    tm = x.shape[0]
    y = x * lax.rsqrt(jnp.mean(x * x, axis=-1, keepdims=True) + EPS) * g_ref[...]
    h = (y * (1.0 + _row_select(sc_ref, tm, n_lat)) + _row_select(sh_ref, tm, n_lat)).astype(o_ref.dtype)
    o_ref[...] = h

    @pl.when(pl.program_id(0) * tm < n_lat)
    def _():
        lat_ref[...] = h


def norm_modulate(x, g, sc, sh, n_lat, out_dtype=BF16, tm=256):
    m, d = x.shape
    vec = pl.BlockSpec((1, d), lambda i: (0, 0))
    two = pl.BlockSpec((2, d), lambda i: (0, 0))
    last_lat = n_lat // tm - 1
    return pl.pallas_call(
        functools.partial(_norm_kernel, n_lat=n_lat),
        grid=(m // tm,),
        in_specs=[pl.BlockSpec((tm, d), lambda i: (i, 0)), vec, two, two],
        out_specs=[pl.BlockSpec((tm, d), lambda i: (i, 0)),
                   pl.BlockSpec((tm, d), lambda i: (jnp.minimum(i, last_lat), 0))],
        out_shape=[jax.ShapeDtypeStruct((m, d), out_dtype), jax.ShapeDtypeStruct((n_lat, d), out_dtype)],
        compiler_params=_cp("arbitrary"),
        name="norm_modulate",
    )(x, g.reshape(1, d), sc, sh)


def _mm_kernel(a_ref, w_ref, o_ref):
    o_ref[...] = jnp.dot(a_ref[...], w_ref[0], preferred_element_type=F32).astype(o_ref.dtype)


def matmul(a, w, layer, col0, n, tm, tn=512, out_dtype=F32):
    m, k = a.shape
    tn = min(tn, n)
    off = col0 // tn
    return pl.pallas_call(
        _mm_kernel,
        grid=(m // tm, n // tn),
        in_specs=[pl.BlockSpec((tm, k), lambda i, j: (i, 0)),
                  pl.BlockSpec((1, k, tn), lambda i, j: (layer, 0, j + off))],
        out_specs=pl.BlockSpec((tm, tn), lambda i, j: (i, j)),
        out_shape=jax.ShapeDtypeStruct((m, n), out_dtype),
        compiler_params=_cp("parallel", "parallel"),
        name="matmul",
    )(a, w)


def _mm_out_kernel(a0_ref, a1_ref, a2_ref, a3_ref, w_ref, r_ref, g_ref, o_ref, *, n_lat):
    tm, kq = a0_ref.shape
    acc = jnp.dot(a0_ref[...], w_ref[0, 0:kq, :], preferred_element_type=F32)
    for q, a_ref in enumerate((a1_ref, a2_ref, a3_ref), start=1):
        acc += jnp.dot(a_ref[...], w_ref[0, q * kq:(q + 1) * kq, :], preferred_element_type=F32)
    o_ref[...] = r_ref[...] + _row_select(g_ref, tm, n_lat) * acc


def matmul_out(parts, w, layer, res, gate, m, n_lat, tm, tn=512):
    kq = parts[0].shape[1]
    n = w.shape[2]
    tn = min(tn, n)
    a_spec = pl.BlockSpec((tm, kq), lambda i, j: (i, 0))
    return pl.pallas_call(
        functools.partial(_mm_out_kernel, n_lat=n_lat),
        grid=(m // tm, n // tn),
        in_specs=[a_spec, a_spec, a_spec, a_spec,
                  pl.BlockSpec((1, 4 * kq, tn), lambda i, j: (layer, 0, j)),
                  pl.BlockSpec((tm, tn), lambda i, j: (i, j)),
                  pl.BlockSpec((2, tn), lambda i, j: (0, j))],
        out_specs=pl.BlockSpec((tm, tn), lambda i, j: (i, j)),
        out_shape=jax.ShapeDtypeStruct((m, n), F32),
        compiler_params=_cp("parallel", "parallel"),
        name="matmul_out",
    )(*parts, w, res, gate)


def _conv3_kernel(x_ref, w_ref, b_ref, *o_refs, act, starts):
    x = x_ref[...]
    n = x.shape[0]
    row = lax.broadcasted_iota(jnp.int32, x.shape, 0)
    first = row == starts[0]
    last = row == n - 1
    for s in starts[1:]:
        first = first | (row == s)
        last = last | (row == s - 1)
    xm = jnp.where(first, 0.0, pltpu.roll(x, 1, 0))
    xp = jnp.where(last, 0.0, pltpu.roll(x, n - 1, 0))
    y = w_ref[0:1, :] * xm + w_ref[1:2, :] * x + w_ref[2:3, :] * xp + b_ref[...]
    if act:
        y = _silu(y)
    if len(o_refs) == 1:
        o_refs[0][...] = y
    else:
        bounds = starts + (n,)
        for o_ref, lo, hi in zip(o_refs, bounds[:-1], bounds[1:]):
            o_ref[...] = y[lo:hi]


def conv3(x, w, b, act, starts=(0,), col0=0, split=False, tc=LANES):
    n, c = x.shape[0], w.shape[1]
    off = col0 // tc
    starts = tuple(s for s in starts if s < n)
    sizes = [hi - lo for lo, hi in zip(starts, starts[1:] + (n,))] if split else [n]
    res = pl.pallas_call(
        functools.partial(_conv3_kernel, act=act, starts=starts),
        grid=(c // tc,),
        in_specs=[pl.BlockSpec((n, tc), lambda j: (0, j + off)),
                  pl.BlockSpec((3, tc), lambda j: (0, j)),
                  pl.BlockSpec((1, tc), lambda j: (0, j))],
        out_specs=[pl.BlockSpec((s, tc), lambda j: (0, j)) for s in sizes],
        out_shape=[jax.ShapeDtypeStruct((s, c), F32) for s in sizes],
        compiler_params=_cp("parallel"),
        name="conv3",
    )(x, w, b.reshape(1, c))
    return res if split else res[0]


def _hyfilt_kernel(fv_ref, w1_ref, b1_ref, fr1_ref, w2_ref, b2_ref, fr2_ref, w3_ref, dl_ref,
                   k_ref, as_ref, *, seq, nfull, tj):
    i = pl.program_id(0)

    def time_of(j):
        tf = jnp.where(j < seq, j, nfull - j).astype(F32)
        return (j < seq) | (j > nfull - seq), tf / (seq - 1), (2.0 * math.pi) * tf / seq

    _, t_row, wpos = time_of(i * tj + lax.broadcasted_iota(jnp.int32, (1, tj), 1))
    nf = fv_ref.shape[0]
    frow = lax.broadcasted_iota(jnp.int32, (nf, tj), 0)
    arg = fv_ref[:, 0:1] * wpos
    feats = jnp.where(frow == 0, t_row,
                      jnp.where(frow <= HY_BANDS, jnp.cos(arg),
                                jnp.where(frow <= 2 * HY_BANDS, -jnp.sin(arg), 0.0)))
    h = jnp.sin(fr1_ref[:, 0:1] * (jnp.dot(w1_ref[...], feats, precision=HI, preferred_element_type=F32)
                                   + b1_ref[:, 0:1]))
    h = jnp.sin(fr2_ref[:, 0:1] * (jnp.dot(w2_ref[...], h, precision=HI, preferred_element_type=F32)
                                   + b2_ref[:, 0:1]))
    h = lax.dot_general(h.astype(BF16), w3_ref[0].astype(BF16), (((0,), (0,)), ((), ())),
                        preferred_element_type=F32)
    valid, t_col, _ = time_of(i * tj + lax.broadcasted_iota(jnp.int32, (tj, 1), 0))
    kv = jnp.where(valid, h * jnp.exp(-t_col * dl_ref[...]), 0.0)
    k_ref[...] = kv.astype(k_ref.dtype)

    @pl.when(i == 0)
    def _():
        as_ref[...] = jnp.zeros_like(as_ref)

    as_ref[0:1, :] += jnp.sum(jnp.abs(kv), axis=0, keepdims=True)


def hyena_filter(seq, nfull, w1, b1, fr1, w2, b2, fr2, w3):
    tj = min(512, nfull)
    nblk = nfull // tj
    nf = 40
    f = jnp.linspace(1e-4, HY_BANDS - 1, HY_BANDS, dtype=F32)
    col = lambda v: jnp.broadcast_to(v.astype(F32).reshape(-1, 1), (v.shape[0], LANES))
    fv = col(jnp.concatenate([jnp.zeros((1,), F32), f, f, jnp.zeros((nf - 2 * HY_BANDS - 1,), F32)]))
    w1t = jnp.pad(w1.astype(F32).T, ((0, 0), (0, nf - w1.shape[0])))
    w3r = w3.astype(F32).reshape(HY_FFN, 2, 2, HY_C).transpose(2, 0, 1, 3).reshape(2, HY_FFN, 2 * HY_C)
    deltas = jnp.abs(jnp.linspace(math.log(1e-2) / 1.5, math.log(1e-2) / 0.3, HY_C, dtype=F32))
    dl = jnp.concatenate([deltas, deltas]).reshape(1, 2 * HY_C)
    small = lambda r, c: pl.BlockSpec((r, c), lambda i: (0, 0))
    return pl.pallas_call(
        functools.partial(_hyfilt_kernel, seq=seq, nfull=nfull, tj=tj),
        grid=(nblk,),
        in_specs=[small(nf, LANES), small(HY_FFN, nf), small(HY_FFN, LANES), small(HY_FFN, LANES),
                  small(HY_FFN, HY_FFN), small(HY_FFN, LANES), small(HY_FFN, LANES),
                  pl.BlockSpec((1, HY_FFN, 2 * HY_C), lambda i: ((2 * i) // nblk, 0, 0)),
                  small(1, 2 * HY_C)],
        out_specs=[pl.BlockSpec((tj, 2 * HY_C), lambda i: (i, 0)),
                   pl.BlockSpec((8, 2 * HY_C), lambda i: (0, 0))],
        out_shape=[jax.ShapeDtypeStruct((nfull, 2 * HY_C), BF16),
                   jax.ShapeDtypeStruct((8, 2 * HY_C), F32)],
        compiler_params=_cp("arbitrary"),
        name="hyena_filter",
    )(fv, w1t, col(b1), col(fr1), w2.astype(F32).T, col(b2), col(fr2), w3r, dl)


def _spec_rows(n1):
    return -(-(n1 // 2 + 1) // 16) * 16


def _dft_tables(n1, rows_in, rows_out):
    nfull = n1 * LANES
    nr = _spec_rows(n1)
    k1 = np.arange(nr)[:, None]
    keep = (k1 <= n1 // 2).astype(np.float64)
    a = np.arange(n1)[None, :]
    th = 2.0 * np.pi * k1 * a / n1
    frow = np.concatenate([np.cos(th) * keep, -np.sin(th) * keep], axis=0)[:, :rows_in]
    b = np.arange(LANES)[None, :]
    tw = 2.0 * np.pi * k1 * b / nfull
    twr, twi = np.cos(tw), -np.sin(tw)
    ph = 2.0 * np.pi * np.arange(LANES)[:, None] * np.arange(LANES)[None, :] / LANES
    fr, fi = np.cos(ph), -np.sin(ph)
    f2 = np.block([[fr, fi], [-fi, fr]])
    f2i = np.block([[fr, -fi], [fi, fr]])
    kk = np.arange(nr)[None, :]
    wgt = np.where((kk == 0) | (kk == n1 // 2), 1.0, np.where(kk < n1 // 2, 2.0, 0.0))
    thi = 2.0 * np.pi * np.arange(rows_out)[:, None] * kk / n1
    finv = np.concatenate([np.cos(thi) * wgt, -np.sin(thi) * wgt], axis=1) / nfull
    c = lambda m, dt: jnp.asarray(m, dtype=dt)
    return dict(frow=c(frow, BF16), twr=c(twr, F32), twi=c(twi, F32), f2=c(f2, BF16), f2i=c(f2i, BF16),
                finv=c(finv, BF16))


def _hy_forward(z_bf, frow_ref, twr_ref, twi_ref, pbuf, qbuf, *, cb, n1, scale=None):
    p = jnp.dot(frow_ref[...], z_bf, preferred_element_type=F32)
    pbuf[...] = p if scale is None else p * scale
    twr, twi = twr_ref[...], twi_ref[...]
    for c in range(cb):
        pr = pbuf[0:n1, c * LANES:(c + 1) * LANES]
        pi = pbuf[n1:2 * n1, c * LANES:(c + 1) * LANES]
        qbuf[c * n1:(c + 1) * n1, 0:LANES] = (pr * twr - pi * twi).astype(BF16)
        qbuf[c * n1:(c + 1) * n1, LANES:2 * LANES] = (pr * twi + pi * twr).astype(BF16)


def _hyspec_kernel(k_ref, sc_ref, frow_ref, twr_ref, twi_ref, f2_ref, o_ref, pbuf, qbuf, *, cb, n1):
    _hy_forward(k_ref[...], frow_ref, twr_ref, twi_ref, pbuf, qbuf, cb=cb, n1=n1, scale=sc_ref[...])
    o_ref[...] = jnp.dot(qbuf[...], f2_ref[...], preferred_element_type=F32).astype(o_ref.dtype)


def hyena_spectrum(kmat, scale, tabs, n1, cb):
    nch = kmat.shape[1] // LANES
    nr = _spec_rows(n1)
    full = lambda s: pl.BlockSpec(s, lambda i: (0, 0))
    return pl.pallas_call(
        functools.partial(_hyspec_kernel, cb=cb, n1=nr),
        grid=(nch // cb,),
        in_specs=[pl.BlockSpec((n1, cb * LANES), lambda i: (0, i)),
                  pl.BlockSpec((1, cb * LANES), lambda i: (0, i)),
                  full((2 * nr, n1)), full((nr, LANES)), full((nr, LANES)), full((2 * LANES, 2 * LANES))],
        out_specs=pl.BlockSpec((cb * nr, 2 * LANES), lambda i: (i, 0)),
        out_shape=jax.ShapeDtypeStruct((nch * nr, 2 * LANES), BF16),
        scratch_shapes=[pltpu.VMEM((2 * nr, cb * LANES), F32), pltpu.VMEM((cb * nr, 2 * LANES), BF16)],
        compiler_params=_cp("parallel"),
        name="hyena_spectrum",
    )(kmat, scale, tabs["frow_full"], tabs["twr"], tabs["twi"], tabs["f2"])


def _hyconv_kernel(z_ref, g_ref, ks_ref, bias_ref, frow_ref, twr_ref, twi_ref, f2_ref, f2i_ref, finv_ref,
                   o_ref, pbuf, qbuf, sbuf, *, cb, n1, sub):
    z = z_ref[...]
    _hy_forward(z.astype(BF16), frow_ref, twr_ref, twi_ref, pbuf, qbuf, cb=cb, n1=n1)
    twr, twi = twr_ref[...], twi_ref[...]
    rows = sub * n1
    for s in range(cb // sub):
        rs = slice(s * rows, (s + 1) * rows)
        zz = jnp.dot(qbuf[rs, :], f2_ref[...], preferred_element_type=F32)
        zr, zi = zz[:, 0:LANES], zz[:, LANES:]
        kr, ki = ks_ref[rs, 0:LANES].astype(F32), ks_ref[rs, LANES:2 * LANES].astype(F32)
        y = jnp.concatenate([zr * kr - zi * ki, zr * ki + zi * kr], axis=1).astype(BF16)
        rr = jnp.dot(y, f2i_ref[...], preferred_element_type=F32)
        for cc in range(sub):
            c = s * sub + cc
            r_re = rr[cc * n1:(cc + 1) * n1, 0:LANES]
            r_im = rr[cc * n1:(cc + 1) * n1, LANES:]
            sbuf[0:n1, c * LANES:(c + 1) * LANES] = (r_re * twr + r_im * twi).astype(BF16)
            sbuf[n1:2 * n1, c * LANES:(c + 1) * LANES] = (r_im * twr - r_re * twi).astype(BF16)
    y = jnp.dot(finv_ref[...], sbuf[...], preferred_element_type=F32)
    o_ref[...] = (y + z * bias_ref[...]) * g_ref[...]


def hyena_conv(z, gate, kspec, order, bias, tabs, n1, cb, sub):
    rows = z[0].shape[0]
    nch, width = HY_C, HY_C * LANES
    nr = _spec_rows(n1)
    koff = order * (nch // cb)
    full = lambda s: pl.BlockSpec(s, lambda i: (0, 0))
    io = pl.BlockSpec((rows, cb * LANES), lambda i: (0, i))
    sel = lambda a: pl.BlockSpec((rows, cb * LANES), lambda i: (0, i + a[1] // cb))
    return pl.pallas_call(
        functools.partial(_hyconv_kernel, cb=cb, n1=nr, sub=sub),
        grid=(nch // cb,),
        in_specs=[sel(z), sel(gate),
                  pl.BlockSpec((cb * nr, 2 * LANES), lambda i: (i + koff, 0)),
                  pl.BlockSpec((1, cb * LANES), lambda i: (0, i)),
                  full((2 * nr, rows)), full((nr, LANES)), full((nr, LANES)),
                  full((2 * LANES, 2 * LANES)), full((2 * LANES, 2 * LANES)), full((rows, 2 * nr))],
        out_specs=io,
        out_shape=jax.ShapeDtypeStruct((rows, width), F32),
        scratch_shapes=[pltpu.VMEM((2 * nr, cb * LANES), F32), pltpu.VMEM((cb * nr, 2 * LANES), BF16),
                        pltpu.VMEM((2 * nr, cb * LANES), BF16)],
        compiler_params=_cp("parallel"),
        name="hyena_conv",
    )(z[0], gate[0], kspec, bias, tabs["frow"], tabs["twr"], tabs["twi"], tabs["f2"], tabs["f2i"], tabs["finv"])


def _to_ab(x, rows):
    n, c = x.shape
    y = x.reshape(n // LANES, LANES, c).transpose(0, 2, 1).reshape(n // LANES, c * LANES)
    return jnp.pad(y, ((0, rows - n // LANES), (0, 0)))


def _from_ab(y, n):
    c = y.shape[1] // LANES
    return y[:n // LANES].reshape(n // LANES, c, LANES).transpose(0, 2, 1).reshape(n, c)


def hyena_mixer(xc, filt, bias):
    seq = xc.shape[0]
    if seq >= 1024:
        n1, rows, cb, sub = 2 * seq // LANES, seq // LANES, 32, 2
    else:
        n1, rows, cb, sub = 16, 16, 128, 16
    nfull = n1 * LANES
    tabs = _dft_tables(n1, rows, rows)
    tabs["frow_full"] = _dft_tables(n1, n1, rows)["frow"]
    kfull, asum = hyena_filter(seq, nfull, *filt)
    scale = jnp.repeat(1.0 / (asum[0] + EPS), LANES).reshape(1, -1)
    kspec = hyena_spectrum(_to_ab(kfull, n1), scale, tabs, n1, cb)
    xab = _to_ab(xc, rows)
    bias_l = jnp.repeat(bias.astype(F32), LANES, axis=1)
    z = hyena_conv((xab, 2 * HY_C), (xab, 0), kspec, 0, bias_l[0:1], tabs, n1, cb, sub)
    y = hyena_conv((z, 0), (xab, HY_C), kspec, 1, bias_l[1:2], tabs, n1, cb, sub)
    return _from_ab(y, seq)


def _dscan_kernel(*refs, rev, n_groups, upg, hpu, wd, use_kap):
    if use_kap:
        q_ref, k_ref, v_ref, la_ref, kap_ref, y_ref, st_ref = refs
    else:
        q_ref, k_ref, v_ref, la_ref, y_ref, st_ref = refs
    i = pl.program_id(0)

    @pl.when(i == 0)
    def _():
        st_ref[...] = jnp.zeros_like(st_ref)

    c = SCAN_CHUNK
    row = lax.broadcasted_iota(jnp.int32, (c, c), 0)
    col = lax.broadcasted_iota(jnp.int32, (c, c), 1)
    mask = (row <= col) if rev else (row >= col)
    tri = jnp.where(mask, 1.0, 0.0).astype(BF16)
    lane = lax.broadcasted_iota(jnp.int32, (c, wd), 1)
    lane1 = lax.broadcasted_iota(jnp.int32, (1, wd), 1)
    sub = wd // hpu

    def lanesel(cols, ln):
        out = cols[-1]
        for hh in range(hpu - 2, -1, -1):
            out = jnp.where(ln < (hh + 1) * sub, cols[hh], out)
        return out

    def one_chunk(rs):
        cum = _cumsum_rows(tri, la_ref[rs, :])
        cum_t = cum.T
        c_end = cum[0:1, :] if rev else cum[c - 1:c, :]
        e_all = jnp.exp(cum)
        w_all = jnp.exp(c_end - cum)
        e_end = jnp.exp(c_end)
        if use_kap:
            kap = kap_ref[rs, :]
            w_all = w_all * kap
        for g in range(n_groups):
            qg = q_ref[rs, g * LANES:(g + 1) * LANES].astype(BF16)
            kg = k_ref[rs, g * LANES:(g + 1) * LANES]
            gm = lax.dot_general(qg, kg.astype(BF16), (((1,), (1,)), ((), ())), preferred_element_type=F32)
            kg_t = kg.T.astype(BF16)
            for uu in range(upg):
                u = g * upg + uu
                v = v_ref[rs, u * wd:(u + 1) * wd]
                heads = [u * hpu + hh for hh in range(hpu)]
                st = st_ref[u]
                y = lanesel([e_all[:, h:h + 1] for h in heads], lane) * jnp.dot(
                    qg, st.astype(BF16), preferred_element_type=F32)
                for hh, h in enumerate(heads):
                    dmat = jnp.exp(jnp.where(mask, cum[:, h:h + 1] - cum_t[h:h + 1, :], NEG))
                    vh = v * kap[:, h:h + 1] if use_kap else v
                    if hpu > 1:
                        vh = jnp.where((lane >= hh * sub) & (lane < (hh + 1) * sub), vh, 0.0)
                    y = y + jnp.dot((gm * dmat).astype(BF16), vh.astype(BF16), preferred_element_type=F32)
                y_ref[rs, u * wd:(u + 1) * wd] = y
                vw = v * lanesel([w_all[:, h:h + 1] for h in heads], lane)
                st_ref[u] = (lanesel([e_end[:, h:h + 1] for h in heads], lane1) * st
                             + jnp.dot(kg_t, vw.astype(BF16), preferred_element_type=F32))

    n_sub = q_ref.shape[0] // c
    for s in (reversed(range(n_sub)) if rev else range(n_sub)):
        one_chunk(slice(s * c, (s + 1) * c))


def _chunk_order(n, n_lat, c, rev):
    nl, nt = n_lat // c, n // c
    if rev:
        return lambda i: nt - 1 - i
    return lambda i: jnp.where(i < nt - nl, nl + i, i - (nt - nl))


def decay_scan(q, k, v, la, kap, n_lat, *, rev, n_groups, upg, hpu, wd):
    n = q[0].shape[0]
    c = SCAN_STEP_CHUNKS * SCAN_CHUNK
    n_units = n_groups * upg
    rmap = _chunk_order(n, n_lat, c, rev)

    def spec(width, blk):
        return pl.BlockSpec((c, width), lambda i: (rmap(i), blk))

    ins = [q, k, v, la] + ([kap] if kap is not None else [])
    widths = [n_groups * LANES, n_groups * LANES, n_units * wd, LANES] + ([LANES] if kap is not None else [])
    return pl.pallas_call(
        functools.partial(_dscan_kernel, rev=rev, n_groups=n_groups, upg=upg, hpu=hpu, wd=wd,
                          use_kap=kap is not None),
        grid=(n // c,),
        in_specs=[spec(w, a[1]) for w, a in zip(widths, ins)],
        out_specs=pl.BlockSpec((c, n_units * wd), lambda i: (rmap(i), 0)),
        out_shape=jax.ShapeDtypeStruct((n, n_units * wd), F32),
        scratch_shapes=[pltpu.VMEM((n_units, LANES, wd), F32)],
        compiler_params=_cp("arbitrary"),
        name="decay_scan_rev" if rev else "decay_scan_fwd",
    )(*[a[0] for a in ins])


def _post_kernel(*refs, n_terms, skip, pre_gate, ngroups):
    terms = refs[:n_terms]
    pos = n_terms
    y = terms[0][...]
    for t in terms[1:]:
        y = y + t[...]
    if skip:
        y = y + refs[pos][...] * refs[pos + 1][...]
        pos += 2
    gate_ref, ng_ref, o_ref = refs[pos:pos + 3]
    gate = _silu(gate_ref[...])
    if pre_gate:
        y = y * gate
    gw = y.shape[1] // ngroups
    for g in range(ngroups):
        sl = slice(g * gw, (g + 1) * gw)
        yg = y[:, sl]
        o = yg * lax.rsqrt(jnp.mean(yg * yg, axis=-1, keepdims=True) + EPS) * ng_ref[:, sl]
        if not pre_gate:
            o = o * gate[:, sl]
        o_ref[:, sl] = o.astype(o_ref.dtype)


def mixer_post(terms, skip, gate, norm_g, *, pre_gate, ngroups, width=1024, tm=256):
    n = terms[0][0].shape[0]
    tm = min(tm, n)
    blk = lambda a: pl.BlockSpec((tm, width), lambda i: (i, a[1]))
    vec = pl.BlockSpec((1, width), lambda i: (0, 0))
    arrays = [t[0] for t in terms]
    specs = [blk(t) for t in terms]
    if skip is not None:
        arrays += [skip[0][0], skip[1].reshape(1, width)]
        specs += [blk(skip[0]), vec]
    arrays += [gate[0], norm_g.reshape(1, width)]
    specs += [blk(gate), vec]
    return pl.pallas_call(
        functools.partial(_post_kernel, n_terms=len(terms), skip=skip is not None, pre_gate=pre_gate,
                          ngroups=ngroups),
        grid=(n // tm,),
        in_specs=specs,
        out_specs=pl.BlockSpec((tm, width), lambda i: (i, 0)),
        out_shape=jax.ShapeDtypeStruct((n, width), BF16),
        compiler_params=_cp("parallel"),
        name="mixer_post",
    )(*arrays)


def _ssd_dt_kernel(x_ref, b_ref, a_ref, dt_ref, la_ref):
    x = x_ref[...] + b_ref[...]
    dt = jnp.maximum(x, 0.0) + jnp.log1p(jnp.exp(-jnp.abs(x)))
    dt_ref[...] = dt
    la_ref[...] = dt * a_ref[...]


def ssd_mixer(u, dt_raw, conv_w, conv_b, dt_bias, a_log, d_skip, norm_g, n_lat):
    n = u.shape[0]
    xbc = conv3(u, conv_w, conv_b, True, starts=(0, n_lat), col0=SSD_INNER)
    pad = lambda t: jnp.pad(t.astype(F32), ((0, 0), (0, LANES - SSD_HEADS))).reshape(1, 2 * LANES)
    tm = 256
    dt, la = pl.pallas_call(
        _ssd_dt_kernel,
        grid=(n // tm,),
        in_specs=[pl.BlockSpec((tm, 2 * LANES), lambda i: (i, 0)),
                  pl.BlockSpec((1, 2 * LANES), lambda i: (0, 0)),
                  pl.BlockSpec((1, 2 * LANES), lambda i: (0, 0))],
        out_specs=[pl.BlockSpec((tm, 2 * LANES), lambda i: (i, 0))] * 2,
        out_shape=[jax.ShapeDtypeStruct((n, 2 * LANES), F32)] * 2,
        compiler_params=_cp("parallel"),
        name="ssd_dt",
    )(dt_raw, pad(dt_bias), pad(-jnp.exp(a_log.astype(F32))))
    ys = [(decay_scan((xbc, 3), (xbc, 2), (xbc, 0), (la, d), (dt, d), n_lat,
                      rev=bool(d), n_groups=SSD_GROUPS, upg=2, hpu=2, wd=LANES), 0) for d in range(2)]
    dsk = jnp.repeat(d_skip.astype(F32), SSD_P)
    return mixer_post(ys, ((xbc, 0), dsk), (u, 0), norm_g, pre_gate=True, ngroups=SSD_GROUPS)


def _rope_kernel(q_ref, k_ref, inv_ref, qo_ref, ko_ref, *, n_lat, n_ctx):
    tm = q_ref.shape[0]
    row = pl.program_id(0) * tm + lax.broadcasted_iota(jnp.int32, (tm, 1), 0)
    pos = jnp.where(row < n_lat, row + n_ctx, row - n_lat).astype(F32)
    ang = pos * inv_ref[...]
    cs, sn = jnp.cos(ang), jnp.sin(ang)
    lane = lax.broadcasted_iota(jnp.int32, (tm, LANES), 1)
    sgn = jnp.where(lane < RET_DK // 2, -sn, sn)
    for h in range(RET_HEADS):
        sl = slice(h * RET_DK, (h + 1) * RET_DK)
        for src, dst, scale in ((q_ref, qo_ref, 1.0), (k_ref, ko_ref, RET_DK ** -0.5)):
            x = src[:, sl]
            y = x * cs + pltpu.roll(x, RET_DK // 2, 1) * sgn
            dst[:, sl] = y * scale


def retention_mixer(u, norm_g, n_lat):
    n = u.shape[0]
    tm = 256
    half = RET_DK // 2
    inv = ROPE_BASE ** (-jnp.arange(half, dtype=F32) / half)
    inv2 = jnp.concatenate([inv, inv]).reshape(1, LANES)
    wq = RET_HEADS * RET_DK
    qr, kr = pl.pallas_call(
        functools.partial(_rope_kernel, n_lat=n_lat, n_ctx=n - n_lat),
        grid=(n // tm,),
        in_specs=[pl.BlockSpec((tm, wq), lambda i: (i, 0)), pl.BlockSpec((tm, wq), lambda i: (i, 1)),
                  pl.BlockSpec((1, LANES), lambda i: (0, 0))],
        out_specs=[pl.BlockSpec((tm, wq), lambda i: (i, 0))] * 2,
        out_shape=[jax.ShapeDtypeStruct((n, wq), F32)] * 2,
        compiler_params=_cp("parallel"),
        name="rope",
    )(u, u, inv2)
    log_gamma = jnp.log1p(-jnp.exp2(-5.0 - jnp.arange(RET_HEADS, dtype=F32)))
    la = jnp.broadcast_to(jnp.pad(log_gamma, (0, LANES - RET_HEADS)).reshape(1, LANES), (n, LANES))
    ys = [(decay_scan((qr, 0), (kr, 0), (u, 1), (la, 0), None, n_lat,
                      rev=bool(d), n_groups=RET_HEADS, upg=1, hpu=1, wd=RET_DV), 0) for d in range(2)]
    return mixer_post(ys, None, (u, 2), norm_g, pre_gate=False, ngroups=RET_HEADS)


def _anchor_rows(gc, b, rev):
    c, n = gc.shape
    shift = b if rev else b - 1
    if 2 * b >= 8:
        return jnp.concatenate([jnp.broadcast_to(gc[s + shift:s + shift + 1, :], (2 * b, n))
                                for s in range(0, c, 2 * b)], axis=0)
    sub = lax.broadcasted_iota(jnp.int32, (8, n), 0)
    tiles = []
    for t in range(0, c, 8):
        tile = None
        for m in range(8 // (2 * b)):
            r = t + m * 2 * b + shift
            cand = jnp.broadcast_to(gc[r:r + 1, :], (8, n))
            tile = cand if tile is None else jnp.where(sub >= m * 2 * b, cand, tile)
        tiles.append(tile)
    return jnp.concatenate(tiles, axis=0)


def _hg_kernel(q_ref, f_ref, v_ref, lb_ref, y_ref, st_ref, *, rev):
    i = pl.program_id(0)

    @pl.when(i == 0)
    def _():
        st_ref[...] = jnp.zeros_like(st_ref)

    c, width = q_ref.shape
    nh = width // HG_DK
    lb = lb_ref[...]
    qh = _silu(q_ref[...])
    fr = f_ref[...]
    lg = jnp.log(lb + (1.0 - lb) * jax.nn.sigmoid(fr))
    kk = (1.0 - lb) * jax.nn.sigmoid(-fr)
    row = lax.broadcasted_iota(jnp.int32, (c, c), 0)
    col = lax.broadcasted_iota(jnp.int32, (c, c), 1)
    tri = jnp.where((row <= col) if rev else (row >= col), 1.0, 0.0).astype(BF16)
    gc = _cumsum_rows(tri, lg)
    g_end = gc[0:1, :] if rev else gc[c - 1:c, :]
    qe = (qh * jnp.exp(gc)).astype(BF16)
    kw = (kk * jnp.exp(g_end - gc)).astype(BF16)
    dec = jnp.exp(g_end)
    vb = v_ref[...].astype(BF16)
    qb, kb = qh.astype(BF16), kk.astype(BF16)
    nt = (((1,), (1,)), ((), ()))
    hs = [slice(h * HG_DK, (h + 1) * HG_DK) for h in range(nh)]
    scores = [jnp.where(row == col, lax.dot_general(qb[:, s], kb[:, s], nt, preferred_element_type=F32), 0.0)
              for s in hs]
    ridx = lax.broadcasted_iota(jnp.int32, (c, 1), 0)
    lvl = 0
    while (1 << lvl) < c:
        b = 1 << lvl
        anchor = _anchor_rows(gc, b, rev)
        upper = ((ridx >> lvl) & 1) == 1
        q_on, k_on = (~upper, upper) if rev else (upper, ~upper)
        qt = (qh * jnp.exp(jnp.where(q_on, gc - anchor, NEG))).astype(BF16)
        kt = (kk * jnp.exp(jnp.where(k_on, anchor - gc, NEG))).astype(BF16)
        same = (row >> (lvl + 1)) == (col >> (lvl + 1))
        for h, s in enumerate(hs):
            sc = lax.dot_general(qt[:, s], kt[:, s], nt, preferred_element_type=F32)
            scores[h] = scores[h] + jnp.where(same, sc, 0.0)
        lvl += 1
    for h, s in enumerate(hs):
        st = st_ref[h]
        y = jnp.dot(scores[h].astype(BF16), vb[:, s], preferred_element_type=F32)
        y = y + lax.dot_general(qe[:, s], st.astype(BF16), nt, preferred_element_type=F32)
        y_ref[:, s] = y
        st_ref[h] = dec[:, s] * st + lax.dot_general(vb[:, s], kw[:, s], (((0,), (0,)), ((), ())),
                                                     preferred_element_type=F32)


def hgrn2_scan(u, lb, n_lat, *, rev, d):
    n = u.shape[0]
    c = HG_CHUNK
    rmap = _chunk_order(n, n_lat, c, rev)
    spec = lambda blk: pl.BlockSpec((c, HG_W), lambda i: (rmap(i), blk))
    return pl.pallas_call(
        functools.partial(_hg_kernel, rev=rev),
        grid=(n // c,),
        in_specs=[spec(0), spec(1 + d), spec(3), pl.BlockSpec((1, HG_W), lambda i: (0, 0))],
        out_specs=pl.BlockSpec((c, HG_W), lambda i: (rmap(i), 0)),
        out_shape=jax.ShapeDtypeStruct((n, HG_W), F32),
        scratch_shapes=[pltpu.VMEM((HG_HEADS, HG_DK, HG_DK), F32)],
        compiler_params=_cp("arbitrary"),
        name="hgrn2_scan_rev" if rev else "hgrn2_scan_fwd",
    )(u, u, u, lb)


def hgrn2_mixer(u, lb, norm_g, n_lat):
    ys = [(hgrn2_scan(u, lb[d:d + 1].astype(F32), n_lat, rev=bool(d), d=d), 0) for d in range(2)]
    return mixer_post(ys, None, (u, 4), norm_g, pre_gate=False, ngroups=HG_HEADS)


_HIGH16 = 0xFFFF0000


def _pack_rows(y, o_ref):
    t, d = y.shape
    nj = d // (2 * LANES)
    words = []
    for j in range(nj):
        lo = y[:, 2 * j * LANES:(2 * j + 1) * LANES].astype(BF16).astype(F32)
        hi = y[:, (2 * j + 1) * LANES:(2 * j + 2) * LANES].astype(BF16).astype(F32)
        words.append((pltpu.bitcast(lo, jnp.uint32) >> 16) | (pltpu.bitcast(hi, jnp.uint32) & jnp.uint32(_HIGH16)))
    o_ref[...] = jnp.swapaxes(jnp.stack(words, axis=0), 0, 1).reshape(t * nj, LANES)


def _unpack_rows(p_ref, t, nj):
    return jnp.swapaxes(p_ref[...].reshape(t, nj, LANES), 0, 1)


def _unpack_piece(words, j):
    lo = pltpu.bitcast(words[j] << 16, F32)
    hi = pltpu.bitcast(words[j] & jnp.uint32(_HIGH16), F32)
    return lo, hi


def _router_kernel(x_ref, g_ref, sc_ref, sh_ref, wr_ref, br_ref, h_ref, r_ref, *, n_lat):
    x = x_ref[...]
    tm = x.shape[0]
    y = x * lax.rsqrt(jnp.mean(x * x, axis=-1, keepdims=True) + EPS) * g_ref[...]
    h = y * (1.0 + _row_select(sc_ref, tm, n_lat)) + _row_select(sh_ref, tm, n_lat)
    _pack_rows(h, h_ref)
    h_hi = h.astype(BF16)
    h_lo = (h - h_hi.astype(F32)).astype(BF16)
    lg = (jnp.dot(h_hi, wr_ref[0], preferred_element_type=F32) + jnp.dot(h_hi, wr_ref[1], preferred_element_type=F32)
          + jnp.dot(h_lo, wr_ref[0], preferred_element_type=F32) + br_ref[...])
    lane = lax.broadcasted_iota(jnp.int32, lg.shape, 1).astype(F32)
    big = float(LANES)
    is_g = lane < MOE_GROUPS
    gmax = jnp.max(jnp.where(is_g, lg, -jnp.inf), axis=1, keepdims=True)
    gsel = jnp.min(jnp.where(is_g & (lg == gmax), lane, big), axis=1, keepdims=True)
    pg = 1.0 / jnp.sum(jnp.where(is_g, jnp.exp(lg - gmax), 0.0), axis=1, keepdims=True)
    lo = MOE_GROUPS + MOE_PER_GROUP * gsel
    in_e = (lane >= lo) & (lane < lo + MOE_PER_GROUP)
    v1 = jnp.max(jnp.where(in_e, lg, -jnp.inf), axis=1, keepdims=True)
    i1 = jnp.min(jnp.where(in_e & (lg == v1), lane, big), axis=1, keepdims=True)
    rest = in_e & (lane != i1)
    v2 = jnp.max(jnp.where(rest, lg, -jnp.inf), axis=1, keepdims=True)
    i2 = jnp.min(jnp.where(rest & (lg == v2), lane, big), axis=1, keepdims=True)
    t = jnp.exp(v2 - v1)
    w1 = pg / (1.0 + t)
    w2 = pg * t / (1.0 + t)
    r_ref[...] = jnp.where(lane == 0, i1 - MOE_GROUPS,
                           jnp.where(lane == 1, i2 - MOE_GROUPS,
                                     jnp.where(lane == 2, w1, jnp.where(lane == 3, w2, 0.0))))


def moe_router(x, g, sc, sh, wr, br, m, n_lat, tm=256):
    d = x.shape[1]
    nj = d // (2 * LANES)
    vec = pl.BlockSpec((1, d), lambda i: (0, 0))
    two = pl.BlockSpec((2, d), lambda i: (0, 0))
    return pl.pallas_call(
        functools.partial(_router_kernel, n_lat=n_lat),
        grid=(m // tm,),
        in_specs=[pl.BlockSpec((tm, d), lambda i: (i, 0)), vec, two, two,
                  pl.BlockSpec((2, d, LANES), lambda i: (0, 0, 0)), pl.BlockSpec((1, LANES), lambda i: (0, 0))],
        out_specs=[pl.BlockSpec((tm * nj, LANES), lambda i: (i, 0)), pl.BlockSpec((tm, LANES), lambda i: (i, 0))],
        out_shape=[jax.ShapeDtypeStruct((m * nj, LANES), jnp.uint32), jax.ShapeDtypeStruct((m, LANES), F32)],
        compiler_params=_cp("parallel"),
        name="moe_router",
    )(x, g.reshape(1, d), sc, sh, wr, br)


def _moe_plan(route, m, n_tiles):
    t = MOE_TILE
    pe = jnp.concatenate([route[:, 0], route[:, 1]]).astype(jnp.int32)
    onehot = (pe[:, None] == jnp.arange(MOE_EXPERTS, dtype=jnp.int32)[None, :]).astype(jnp.int32)
    csum = jnp.cumsum(onehot, axis=0)
    rank = jnp.take_along_axis(csum, pe[:, None], axis=1)[:, 0] - 1
    cnt = csum[-1]
    ntile = (cnt + t - 1) // t
    tend = jnp.cumsum(ntile)
    tstart = tend - ntile
    slot = tstart[pe] * t + rank
    tid = jnp.arange(n_tiles, dtype=jnp.int32)
    tile_e = jnp.clip(jnp.searchsorted(tend, tid, side="right"), 0, MOE_EXPERTS - 1).astype(jnp.int32)
    tile_valid = jnp.where(tid < tend[-1], jnp.clip(cnt[tile_e] - (tid - tstart[tile_e]) * t, 0, t), 0)
    tok = jnp.tile(jnp.arange(m, dtype=jnp.int32), 2)
    row_tok = jnp.zeros((n_tiles * t,), jnp.int32).at[slot].set(tok)
    return row_tok, tile_e, tile_valid.astype(jnp.int32), slot[:m], slot[m:]


def _row_copies(idx_ref, base, src_hbm, dst, sem, rows, nj, wait):
    if wait:
        pltpu.make_async_copy(src_hbm.at[pl.ds(0, rows * nj), :], dst, sem).wait()
        return

    def body(r, carry):
        src = idx_ref[base + r] * nj
        pltpu.make_async_copy(src_hbm.at[pl.ds(src, nj), :], dst.at[pl.ds(r * nj, nj), :], sem).start()
        return carry
    lax.fori_loop(0, rows, body, 0, unroll=8)


def _expert_kernel(te_ref, tv_ref, tok_ref, hp_hbm, w1_ref, w3_ref, w2_ref, o_ref, buf, xbuf, sem, *, nj):
    i = pl.program_id(0)
    n = pl.num_programs(0)
    t = xbuf.shape[0]

    def fetch(tile, slot):
        _row_copies(tok_ref, tile * t, hp_hbm, buf.at[slot], sem.at[slot], t, nj, wait=False)

    @pl.when((i == 0) & (tv_ref[0] > 0))
    def _():
        fetch(0, 0)

    @pl.when((i + 1 < n) & (tv_ref[jnp.minimum(i + 1, n - 1)] > 0))
    def _():
        fetch(i + 1, (i + 1) % 2)

    @pl.when(tv_ref[i] > 0)
    def _():
        slot = i % 2
        cur = buf.at[slot]
        _row_copies(tok_ref, 0, hp_hbm, cur, sem.at[slot], t, nj, wait=True)
        words = _unpack_rows(cur, t, nj)
        for j in range(nj):
            lo, hi = _unpack_piece(words, j)
            xbuf[:, 2 * j * LANES:(2 * j + 1) * LANES] = lo.astype(BF16)
            xbuf[:, (2 * j + 1) * LANES:(2 * j + 2) * LANES] = hi.astype(BF16)
        x = xbuf[...]
        a = jnp.dot(x, w1_ref[0, 0], preferred_element_type=F32)
        b = jnp.dot(x, w3_ref[0, 0], preferred_element_type=F32)
        y = jnp.dot((_silu(a) * b).astype(BF16), w2_ref[0, 0], preferred_element_type=F32)
        _pack_rows(y, o_ref)

    @pl.when(tv_ref[i] == 0)
    def _():
        o_ref[...] = jnp.zeros_like(o_ref)


def moe_experts(hp, row_tok, tile_e, tile_valid, w1, w3, w2, layer):
    t = MOE_TILE
    n_tiles = tile_e.shape[0]
    d, ff = w1.shape[2], w1.shape[3]
    nj = d // (2 * LANES)
    wmap = lambda i, te, tv, tok: (layer, te[i], 0, 0)
    return pl.pallas_call(
        functools.partial(_expert_kernel, nj=nj),
        grid_spec=pltpu.PrefetchScalarGridSpec(
            num_scalar_prefetch=3, grid=(n_tiles,),
            in_specs=[pl.BlockSpec(memory_space=pl.ANY),
                      pl.BlockSpec((1, 1, d, ff), wmap), pl.BlockSpec((1, 1, d, ff), wmap),
                      pl.BlockSpec((1, 1, ff, d), wmap)],
            out_specs=pl.BlockSpec((t * nj, LANES), lambda i, te, tv, tok: (i, 0)),
            scratch_shapes=[pltpu.VMEM((2, t * nj, LANES), jnp.uint32), pltpu.VMEM((t, d), BF16),
                            pltpu.SemaphoreType.DMA((2,))]),
        out_shape=jax.ShapeDtypeStruct((n_tiles * t * nj, LANES), jnp.uint32),
        compiler_params=_cp("arbitrary"),
        name="moe_experts",
    )(tile_e, tile_valid, row_tok, hp, w1, w3, w2)


def _combine_kernel(s0_ref, s1_ref, x_ref, r_ref, g_ref, fg_ref, ys_hbm, o_ref, buf, sem, *, n_lat, final, nj):
    tt = x_ref.shape[0]
    i = pl.program_id(0)
    n = pl.num_programs(0)
    base = i * tt

    def fetch(step, slot):
        for k, s_ref in enumerate((s0_ref, s1_ref)):
            _row_copies(s_ref, step * tt, ys_hbm, buf.at[slot, k], sem.at[slot, k], tt, nj, wait=False)

    @pl.when(i == 0)
    def _():
        fetch(0, 0)

    @pl.when(i + 1 < n)
    def _():
        fetch(i + 1, (i + 1) % 2)

    slot = i % 2
    for k in range(2):
        _row_copies(s0_ref, 0, ys_hbm, buf.at[slot, k], sem.at[slot, k], tt, nj, wait=True)
    r = r_ref[...]
    w0 = jnp.broadcast_to(r[:, 2:3], (tt, LANES))
    w1 = jnp.broadcast_to(r[:, 3:4], (tt, LANES))
    is_ctx = base >= n_lat
    words0 = _unpack_rows(buf.at[slot, 0], tt, nj)
    words1 = _unpack_rows(buf.at[slot, 1], tt, nj)
    for j in range(nj):
        p0 = _unpack_piece(words0, j)
        p1 = _unpack_piece(words1, j)
        for half in range(2):
            cs = slice((2 * j + half) * LANES, (2 * j + half + 1) * LANES)
            gate = jnp.where(is_ctx, g_ref[1:2, cs], g_ref[0:1, cs])
            o_ref[:, cs] = x_ref[:, cs] + gate * (w0 * p0[half] + w1 * p1[half])
    if final:
        y = o_ref[...]
        o_ref[...] = y * lax.rsqrt(jnp.mean(y * y, axis=-1, keepdims=True) + EPS) * fg_ref[...]


def moe_combine(x, route, gate, final_g, ys, slot0, slot1, m, n_lat, final, tt=256):
    d = x.shape[1]
    nj = d // (2 * LANES)
    assert n_lat % tt == 0
    return pl.pallas_call(
        functools.partial(_combine_kernel, n_lat=n_lat, final=final, nj=nj),
        grid_spec=pltpu.PrefetchScalarGridSpec(
            num_scalar_prefetch=2, grid=(m // tt,),
            in_specs=[pl.BlockSpec((tt, d), lambda i, a, b: (i, 0)),
                      pl.BlockSpec((tt, LANES), lambda i, a, b: (i, 0)),
                      pl.BlockSpec((2, d), lambda i, a, b: (0, 0)),
                      pl.BlockSpec((1, d), lambda i, a, b: (0, 0)),
                      pl.BlockSpec(memory_space=pl.ANY)],
            out_specs=pl.BlockSpec((tt, d), lambda i, a, b: (i, 0)),
            scratch_shapes=[pltpu.VMEM((2, 2, tt * nj, LANES), jnp.uint32), pltpu.SemaphoreType.DMA((2, 2))]),
        out_shape=jax.ShapeDtypeStruct((m, d), F32),
        compiler_params=_cp("arbitrary"),
        name="moe_combine",
    )(slot0, slot1, x, route, gate, final_g.reshape(1, d), ys)


def hier_moe(x, norm_g, sc, sh, gate, P, final_g, m, n_lat, final):
    n_tiles = 2 * m // MOE_TILE + MOE_EXPERTS
    hp, route = moe_router(x, norm_g, sc, sh, P["moe_wr"], P["moe_br"], m, n_lat)
    row_tok, tile_e, tile_valid, slot0, slot1 = _moe_plan(route, m, n_tiles)
    ys = moe_experts(hp, row_tok, tile_e, tile_valid, P["moe_w1"], P["moe_w3"], P["moe_w2"], P["layer"])
    return moe_combine(x, route, gate, final_g, ys, slot0, slot1, m, n_lat, final)


def _to_cols(t, rows):
    n, ch = t.shape
    return jnp.swapaxes(t.reshape(rows, GRID_W, ch), 0, 1).reshape(n, ch)


def _from_cols(t, rows):
    n, ch = t.shape
    return jnp.swapaxes(t.reshape(GRID_W, rows, ch), 0, 1).reshape(n, ch)


def _layer(xa, mod, P, lb, n_lat, last, final_g):
    n, d = xa.shape
    sh1, sc1, g1, sh2, sc2, g2 = (mod[:, q * d:(q + 1) * d] for q in range(6))
    grid_rows = n_lat // GRID_W
    h, h_lat = norm_modulate(xa, P["norm1"], sc1, sh1, n_lat)
    h_cols = jnp.concatenate([_to_cols(h_lat, grid_rows), h[n_lat:]], axis=0)
    tm = 768 if n % 768 == 0 else 256
    layer, w_in = P["layer"], P["w_in"]
    u_hy = matmul(h, w_in["hy"], layer, 0, 3072, tm)
    u_ssd = matmul(h, w_in["ssd"], layer, 0, 3072, tm)
    u_hg = matmul(h_cols, w_in["hg"], layer, 0, 5120, tm)
    u_ret = matmul(h, w_in["ret"], layer, 0, 3072, tm)
    dt_raw = matmul(h, P["w_dt"], layer, 0, 2 * LANES, tm, tn=2 * LANES)
    filt = (P["hy_w1"], P["hy_b1"], P["hy_fr1"], P["hy_w2"], P["hy_b2"], P["hy_fr2"], P["hy_w3"])
    xc_lat, xc_ctx = conv3(u_hy, P["hy_conv_w"], P["hy_conv_b"], False, starts=(0, n_lat), split=True)
    a = hyena_mixer(xc_lat, filt, P["hy_bias"]).astype(BF16)
    if not last:
        a = jnp.concatenate([a, hyena_mixer(xc_ctx, filt, P["hy_bias"]).astype(BF16)], axis=0)
    b = ssd_mixer(u_ssd, dt_raw, P["ssd_conv_w"], P["ssd_conv_b"], P["ssd_dt_bias"], P["ssd_a_log"],
                  P["ssd_d"], P["ssd_norm_g"], n_lat)
    c = hgrn2_mixer(u_hg, lb, P["hg_norm_g"], n_lat)
    c = jnp.concatenate([_from_cols(c[:n_lat], grid_rows), c[n_lat:]], axis=0)
    dd = retention_mixer(u_ret, P["ret_norm_g"], n_lat)
    m = n_lat if last else n
    xa = matmul_out((a, b, c, dd), P["w_out"], layer, xa, g1, m, n_lat, tm=1024 if m % 1024 == 0 else tm)
    return hier_moe(xa, P["norm2"], sc2, sh2, g2, P, final_g, m, n_lat, last)


def kernel(x, c, ctx, c_ctx, ada_w, ada_b, norm1_g, norm2_g, w_in, w_out, hy_conv_w, hy_conv_b, hy_w1, hy_b1, hy_fr1, hy_w2, hy_b2, hy_fr2, hy_w3, hy_bias, ssd_conv_w, ssd_conv_b, ssd_dt_bias, ssd_a_log, ssd_d, ssd_norm_g, hg_lb_raw, hg_norm_g, ret_norm_g, moe_wg, moe_bg, moe_we, moe_be, moe_w1, moe_w3, moe_w2, final_g):
    depth = ada_w.shape[0]
    n_lat = x.shape[1]
    lb_prob = jax.nn.softmax(hg_lb_raw.astype(F32), axis=0)
    lb_all = jnp.cumsum(lb_prob, axis=0) - lb_prob[0]
    mods = ada_modulation(c, c_ctx, ada_w, ada_b)
    xa = jnp.concatenate([x[0], ctx[0]], axis=0)
    o_ssd, o_dt, o_hg, o_ret = 3072, 6144, 6176, 11296
    w_groups = {"hy": w_in[:, :, :o_ssd].astype(BF16), "ssd": w_in[:, :, o_ssd:o_dt].astype(BF16),
                "hg": w_in[:, :, o_hg:o_ret].astype(BF16), "ret": w_in[:, :, o_ret:].astype(BF16)}
    zpad = jnp.zeros(w_in.shape[:2] + (LANES - SSD_HEADS,), w_in.dtype)
    w_dt_b = jnp.concatenate([w_in[:, :, o_dt:o_dt + SSD_HEADS], zpad,
                              w_in[:, :, o_dt + SSD_HEADS:o_hg], zpad], axis=2).astype(BF16)
    w_out_b = w_out.astype(BF16)
    moe_b = [w.astype(BF16) for w in (moe_w1, moe_w3, moe_w2)]
    for l in range(depth):
        pad_r = LANES - MOE_GROUPS - MOE_EXPERTS
        wr = jnp.pad(jnp.concatenate([moe_wg[l], moe_we[l]], axis=1).astype(F32), ((0, 0), (0, pad_r)))
        wr_hi = wr.astype(BF16)
        wr_lo = (wr - wr_hi.astype(F32)).astype(BF16)
        P = {
            "layer": l, "norm1": norm1_g[l], "norm2": norm2_g[l], "w_in": w_groups, "w_dt": w_dt_b, "w_out": w_out_b,
            "hy_conv_w": hy_conv_w[l], "hy_conv_b": hy_conv_b[l], "hy_w1": hy_w1[l], "hy_b1": hy_b1[l],
            "hy_fr1": hy_fr1[l], "hy_w2": hy_w2[l], "hy_b2": hy_b2[l], "hy_fr2": hy_fr2[l], "hy_w3": hy_w3[l],
            "hy_bias": hy_bias[l],
            "ssd_conv_w": ssd_conv_w[l], "ssd_conv_b": ssd_conv_b[l], "ssd_dt_bias": ssd_dt_bias[l],
            "ssd_a_log": ssd_a_log[l], "ssd_d": ssd_d[l], "ssd_norm_g": ssd_norm_g[l],
            "hg_norm_g": hg_norm_g[l], "ret_norm_g": ret_norm_g[l],
            "moe_wr": jnp.stack([wr_hi, wr_lo]),
            "moe_br": jnp.pad(jnp.concatenate([moe_bg[l], moe_be[l]]).astype(F32), (0, pad_r)).reshape(1, LANES),
            "moe_w1": moe_b[0], "moe_w3": moe_b[1], "moe_w2": moe_b[2],
        }
        xa = _layer(xa, mods[l, 0:2], P, lb_all[l], n_lat, l == depth - 1, final_g)
    return xa[None]
```

```python
import functools
import math

import numpy as np
import jax
import jax.numpy as jnp
from jax import lax
from jax.experimental import pallas as pl
from jax.experimental.pallas import tpu as pltpu

F32 = jnp.float32
BF16 = jnp.bfloat16
EPS = 1e-6
NEG = -1e30
LANES = 128
VMEM_LIMIT = 56 * 1024 * 1024

D_MODEL = 4096
GRID_W = 64
HY_C = 1024
HY_BANDS = 16
HY_FFN = 64
SSD_HEADS = 16
SSD_P = 64
SSD_INNER = SSD_HEADS * SSD_P
SSD_GROUPS = 4
SSD_N = 128
HG_HEADS = 8
HG_DK = 128
HG_W = HG_HEADS * HG_DK
RET_HEADS = 4
RET_DK = 128
RET_DV = 256
RET_W = RET_HEADS * RET_DV
ROPE_BASE = 10000.0
MOE_GROUPS = 4
MOE_PER_GROUP = 8
MOE_EXPERTS = 32
MOE_FF = 512
MOE_TILE = 256
SCAN_CHUNK = 128
SCAN_STEP_CHUNKS = 2
HG_CHUNK = 128
HI = lax.Precision.HIGHEST


def _cp(*sem):
    return pltpu.CompilerParams(dimension_semantics=sem, vmem_limit_bytes=VMEM_LIMIT)


def _silu(x):
    return x * jax.nn.sigmoid(x)


def _split3(x):
    hi = x.astype(BF16)
    r1 = x - hi.astype(F32)
    mid = r1.astype(BF16)
    lo = (r1 - mid.astype(F32)).astype(BF16)
    return hi, mid, lo


def _cumsum_rows(tri, x):
    hi, mid, lo = _split3(x)
    return (jnp.dot(tri, hi, preferred_element_type=F32)
            + jnp.dot(tri, mid, preferred_element_type=F32)
            + jnp.dot(tri, lo, preferred_element_type=F32))


def _ada_kernel(cb_ref, w_ref, b_ref, o_ref, acc_ref, *, nk):
    k = pl.program_id(2)

    @pl.when(k == 0)
    def _():
        acc_ref[...] = jnp.zeros_like(acc_ref)

    tk, tn = w_ref.shape[1], w_ref.shape[2]
    for s in range(2):
        sv = _silu(cb_ref[s])
        for j in range(tn // LANES):
            p = w_ref[0, :, j * LANES:(j + 1) * LANES] * sv
            acc_ref[s, :, j * LANES:(j + 1) * LANES] += p.reshape(tk // 8, 8, LANES).sum(axis=0)

    @pl.when(k == nk - 1)
    def _():
        r0 = acc_ref[0].sum(axis=0, keepdims=True) + b_ref[0]
        r1 = acc_ref[1].sum(axis=0, keepdims=True) + b_ref[0]
        row = lax.broadcasted_iota(jnp.int32, (8, tn), 0)
        o_ref[0] = jnp.where(row == 0, r0, jnp.where(row == 1, r1, 0.0))


def ada_modulation(c, c_ctx, ada_w, ada_b):
    depth, d, n = ada_w.shape
    tk, tn = min(512, d), math.gcd(n, 4096)
    cb = jnp.stack([jnp.broadcast_to(c.reshape(d, 1), (d, LANES)),
                    jnp.broadcast_to(c_ctx.reshape(d, 1), (d, LANES))])
    nk = d // tk
    return pl.pallas_call(
        functools.partial(_ada_kernel, nk=nk),
        grid=(depth, n // tn, nk),
        in_specs=[pl.BlockSpec((2, tk, LANES), lambda l, j, k: (0, k, 0)),
                  pl.BlockSpec((1, tk, tn), lambda l, j, k: (l, k, j)),
                  pl.BlockSpec((1, 1, tn), lambda l, j, k: (l, 0, j))],
        out_specs=pl.BlockSpec((1, 8, tn), lambda l, j, k: (l, 0, j)),
        out_shape=jax.ShapeDtypeStruct((depth, 8, n), F32),
        scratch_shapes=[pltpu.VMEM((2, 8, tn), F32)],
        compiler_params=_cp("arbitrary", "arbitrary", "arbitrary"),
        name="ada_modulation",
    )(cb, ada_w, ada_b.reshape(depth, 1, n))


def _row_select(ref, tm, n_lat):
    row = pl.program_id(0) * tm + lax.broadcasted_iota(jnp.int32, (tm, 1), 0)
    return jnp.where(row < n_lat, ref[0:1, :], ref[1:2, :])


def _norm_kernel(x_ref, g_ref, sc_ref, sh_ref, o_ref, lat_ref, *, n_lat):
    x = x_ref[...]
    tm = x.shape[0]
    y = x * lax.rsqrt(jnp.mean(x * x, axis=-1, keepdims=True) + EPS) * g_ref[...]
    h = (y * (1.0 + _row_select(sc_ref, tm, n_lat)) + _row_select(sh_ref, tm, n_lat)).astype(o_ref.dtype)
    o_ref[...] = h

    @pl.when(pl.program_id(0) * tm < n_lat)
    def _():
        lat_ref[...] = h


def norm_modulate(x, g, sc, sh, n_lat, out_dtype=BF16, tm=256):
    m, d = x.shape
    vec = pl.BlockSpec((1, d), lambda i: (0, 0))
    two = pl.BlockSpec((2, d), lambda i: (0, 0))
    last_lat = n_lat // tm - 1
    return pl.pallas_call(
        functools.partial(_norm_kernel, n_lat=n_lat),
        grid=(m // tm,),
        in_specs=[pl.BlockSpec((tm, d), lambda i: (i, 0)), vec, two, two],
        out_specs=[pl.BlockSpec((tm, d), lambda i: (i, 0)),
                   pl.BlockSpec((tm, d), lambda i: (jnp.minimum(i, last_lat), 0))],
        out_shape=[jax.ShapeDtypeStruct((m, d), out_dtype), jax.ShapeDtypeStruct((n_lat, d), out_dtype)],
        compiler_params=_cp("arbitrary"),
        name="norm_modulate",
    )(x, g.reshape(1, d), sc, sh)


def _mm_kernel(a_ref, w_ref, o_ref):
    o_ref[...] = jnp.dot(a_ref[...], w_ref[0], preferred_element_type=F32).astype(o_ref.dtype)


def matmul(a, w, layer, col0, n, tm, tn=512, out_dtype=F32):
    m, k = a.shape
    tn = min(tn, n)
    off = col0 // tn
    return pl.pallas_call(
        _mm_kernel,
        grid=(m // tm, n // tn),
        in_specs=[pl.BlockSpec((tm, k), lambda i, j: (i, 0)),
                  pl.BlockSpec((1, k, tn), lambda i, j: (layer, 0, j + off))],
        out_specs=pl.BlockSpec((tm, tn), lambda i, j: (i, j)),
        out_shape=jax.ShapeDtypeStruct((m, n), out_dtype),
        compiler_params=_cp("parallel", "parallel"),
        name="matmul",
    )(a, w)


def _mm_out_kernel(a0_ref, a1_ref, a2_ref, a3_ref, w_ref, r_ref, g_ref, o_ref, *, n_lat):
    tm, kq = a0_ref.shape
    acc = jnp.dot(a0_ref[...], w_ref[0, 0:kq, :], preferred_element_type=F32)
    for q, a_ref in enumerate((a1_ref, a2_ref, a3_ref), start=1):
        acc += jnp.dot(a_ref[...], w_ref[0, q * kq:(q + 1) * kq, :], preferred_element_type=F32)
    o_ref[...] = r_ref[...] + _row_select(g_ref, tm, n_lat) * acc


def matmul_out(parts, w, layer, res, gate, m, n_lat, tm, tn=512):
    kq = parts[0].shape[1]
    n = w.shape[2]
    tn = min(tn, n)
    a_spec = pl.BlockSpec((tm, kq), lambda i, j: (i, 0))
    return pl.pallas_call(
        functools.partial(_mm_out_kernel, n_lat=n_lat),
        grid=(m // tm, n // tn),
        in_specs=[a_spec, a_spec, a_spec, a_spec,
                  pl.BlockSpec((1, 4 * kq, tn), lambda i, j: (layer, 0, j)),
                  pl.BlockSpec((tm, tn), lambda i, j: (i, j)),
                  pl.BlockSpec((2, tn), lambda i, j: (0, j))],
        out_specs=pl.BlockSpec((tm, tn), lambda i, j: (i, j)),
        out_shape=jax.ShapeDtypeStruct((m, n), F32),
        compiler_params=_cp("parallel", "parallel"),
        name="matmul_out",
    )(*parts, w, res, gate)


def _conv3_kernel(x_ref, w_ref, b_ref, *o_refs, act, starts):
    x = x_ref[...]
    n = x.shape[0]
    row = lax.broadcasted_iota(jnp.int32, x.shape, 0)
    first = row == starts[0]
    last = row == n - 1
    for s in starts[1:]:
        first = first | (row == s)
        last = last | (row == s - 1)
    xm = jnp.where(first, 0.0, pltpu.roll(x, 1, 0))
    xp = jnp.where(last, 0.0, pltpu.roll(x, n - 1, 0))
    y = w_ref[0:1, :] * xm + w_ref[1:2, :] * x + w_ref[2:3, :] * xp + b_ref[...]
    if act:
        y = _silu(y)
    if len(o_refs) == 1:
        o_refs[0][...] = y
    else:
        bounds = starts + (n,)
        for o_ref, lo, hi in zip(o_refs, bounds[:-1], bounds[1:]):
            o_ref[...] = y[lo:hi]


def conv3(x, w, b, act, starts=(0,), col0=0, split=False, tc=LANES):
    n, c = x.shape[0], w.shape[1]
    off = col0 // tc
    starts = tuple(s for s in starts if s < n)
    sizes = [hi - lo for lo, hi in zip(starts, starts[1:] + (n,))] if split else [n]
    res = pl.pallas_call(
        functools.partial(_conv3_kernel, act=act, starts=starts),
        grid=(c // tc,),
        in_specs=[pl.BlockSpec((n, tc), lambda j: (0, j + off)),
                  pl.BlockSpec((3, tc), lambda j: (0, j)),
                  pl.BlockSpec((1, tc), lambda j: (0, j))],
        out_specs=[pl.BlockSpec((s, tc), lambda j: (0, j)) for s in sizes],
        out_shape=[jax.ShapeDtypeStruct((s, c), F32) for s in sizes],
        compiler_params=_cp("parallel"),
        name="conv3",
    )(x, w, b.reshape(1, c))
    return res if split else res[0]


def _hyfilt_kernel(fv_ref, w1_ref, b1_ref, fr1_ref, w2_ref, b2_ref, fr2_ref, w3_ref, dl_ref,
                   k_ref, as_ref, *, seq, nfull, tj):
    i = pl.program_id(0)

    def time_of(j):
        tf = jnp.where(j < seq, j, nfull - j).astype(F32)
        return (j < seq) | (j > nfull - seq), tf / (seq - 1), (2.0 * math.pi) * tf / seq

    _, t_row, wpos = time_of(i * tj + lax.broadcasted_iota(jnp.int32, (1, tj), 1))
    nf = fv_ref.shape[0]
    frow = lax.broadcasted_iota(jnp.int32, (nf, tj), 0)
    arg = fv_ref[:, 0:1] * wpos
    feats = jnp.where(frow == 0, t_row,
                      jnp.where(frow <= HY_BANDS, jnp.cos(arg),
                                jnp.where(frow <= 2 * HY_BANDS, -jnp.sin(arg), 0.0)))
    h = jnp.sin(fr1_ref[:, 0:1] * (jnp.dot(w1_ref[...], feats, precision=HI, preferred_element_type=F32)
                                   + b1_ref[:, 0:1]))
    h = jnp.sin(fr2_ref[:, 0:1] * (jnp.dot(w2_ref[...], h, precision=HI, preferred_element_type=F32)
                                   + b2_ref[:, 0:1]))
    h = lax.dot_general(h.astype(BF16), w3_ref[0].astype(BF16), (((0,), (0,)), ((), ())),
                        preferred_element_type=F32)
    valid, t_col, _ = time_of(i * tj + lax.broadcasted_iota(jnp.int32, (tj, 1), 0))
    kv = jnp.where(valid, h * jnp.exp(-t_col * dl_ref[...]), 0.0)
    k_ref[...] = kv.astype(k_ref.dtype)

    @pl.when(i == 0)
    def _():
        as_ref[...] = jnp.zeros_like(as_ref)

    as_ref[0:1, :] += jnp.sum(jnp.abs(kv), axis=0, keepdims=True)


def hyena_filter(seq, nfull, w1, b1, fr1, w2, b2, fr2, w3):
    tj = min(512, nfull)
    nblk = nfull // tj
    nf = 40
    f = jnp.linspace(1e-4, HY_BANDS - 1, HY_BANDS, dtype=F32)
    col = lambda v: jnp.broadcast_to(v.astype(F32).reshape(-1, 1), (v.shape[0], LANES))
    fv = col(jnp.concatenate([jnp.zeros((1,), F32), f, f, jnp.zeros((nf - 2 * HY_BANDS - 1,), F32)]))
    w1t = jnp.pad(w1.astype(F32).T, ((0, 0), (0, nf - w1.shape[0])))
    w3r = w3.astype(F32).reshape(HY_FFN, 2, 2, HY_C).transpose(2, 0, 1, 3).reshape(2, HY_FFN, 2 * HY_C)
    deltas = jnp.abs(jnp.linspace(math.log(1e-2) / 1.5, math.log(1e-2) / 0.3, HY_C, dtype=F32))
    dl = jnp.concatenate([deltas, deltas]).reshape(1, 2 * HY_C)
    small = lambda r, c: pl.BlockSpec((r, c), lambda i: (0, 0))
    return pl.pallas_call(
        functools.partial(_hyfilt_kernel, seq=seq, nfull=nfull, tj=tj),
        grid=(nblk,),
        in_specs=[small(nf, LANES), small(HY_FFN, nf), small(HY_FFN, LANES), small(HY_FFN, LANES),
                  small(HY_FFN, HY_FFN), small(HY_FFN, LANES), small(HY_FFN, LANES),
                  pl.BlockSpec((1, HY_FFN, 2 * HY_C), lambda i: ((2 * i) // nblk, 0, 0)),
                  small(1, 2 * HY_C)],
        out_specs=[pl.BlockSpec((tj, 2 * HY_C), lambda i: (i, 0)),
                   pl.BlockSpec((8, 2 * HY_C), lambda i: (0, 0))],
        out_shape=[jax.ShapeDtypeStruct((nfull, 2 * HY_C), BF16),
                   jax.ShapeDtypeStruct((8, 2 * HY_C), F32)],
        compiler_params=_cp("arbitrary"),
        name="hyena_filter",
    )(fv, w1t, col(b1), col(fr1), w2.astype(F32).T, col(b2), col(fr2), w3r, dl)


def _spec_rows(n1):
    return -(-(n1 // 2 + 1) // 16) * 16


def _dft_tables(n1, rows_in, rows_out):
    nfull = n1 * LANES
    nr = _spec_rows(n1)
    k1 = np.arange(nr)[:, None]
    keep = (k1 <= n1 // 2).astype(np.float64)
    a = np.arange(n1)[None, :]
    th = 2.0 * np.pi * k1 * a / n1
    frow = np.concatenate([np.cos(th) * keep, -np.sin(th) * keep], axis=0)[:, :rows_in]
    b = np.arange(LANES)[None, :]
    tw = 2.0 * np.pi * k1 * b / nfull
    twr, twi = np.cos(tw), -np.sin(tw)
    ph = 2.0 * np.pi * np.arange(LANES)[:, None] * np.arange(LANES)[None, :] / LANES
    fr, fi = np.cos(ph), -np.sin(ph)
    f2 = np.block([[fr, fi], [-fi, fr]])
    f2i = np.block([[fr, -fi], [fi, fr]])
    kk = np.arange(nr)[None, :]
    wgt = np.where((kk == 0) | (kk == n1 // 2), 1.0, np.where(kk < n1 // 2, 2.0, 0.0))
    thi = 2.0 * np.pi * np.arange(rows_out)[:, None] * kk / n1
    finv = np.concatenate([np.cos(thi) * wgt, -np.sin(thi) * wgt], axis=1) / nfull
    c = lambda m, dt: jnp.asarray(m, dtype=dt)
    return dict(frow=c(frow, BF16), twr=c(twr, F32), twi=c(twi, F32), f2=c(f2, BF16), f2i=c(f2i, BF16),
                finv=c(finv, BF16))


def _hy_forward(z_bf, frow_ref, twr_ref, twi_ref, pbuf, qbuf, *, cb, n1, scale=None):
    p = jnp.dot(frow_ref[...], z_bf, preferred_element_type=F32)
    pbuf[...] = p if scale is None else p * scale
    twr, twi = twr_ref[...], twi_ref[...]
    for c in range(cb):
        pr = pbuf[0:n1, c * LANES:(c + 1) * LANES]
        pi = pbuf[n1:2 * n1, c * LANES:(c + 1) * LANES]
        qbuf[c * n1:(c + 1) * n1, 0:LANES] = (pr * twr - pi * twi).astype(BF16)
        qbuf[c * n1:(c + 1) * n1, LANES:2 * LANES] = (pr * twi + pi * twr).astype(BF16)


def _hyspec_kernel(k_ref, sc_ref, frow_ref, twr_ref, twi_ref, f2_ref, o_ref, pbuf, qbuf, *, cb, n1):
    _hy_forward(k_ref[...], frow_ref, twr_ref, twi_ref, pbuf, qbuf, cb=cb, n1=n1, scale=sc_ref[...])
    o_ref[...] = jnp.dot(qbuf[...], f2_ref[...], preferred_element_type=F32).astype(o_ref.dtype)


def hyena_spectrum(kmat, scale, tabs, n1, cb):
    nch = kmat.shape[1] // LANES
    nr = _spec_rows(n1)
    full = lambda s: pl.BlockSpec(s, lambda i: (0, 0))
    return pl.pallas_call(
        functools.partial(_hyspec_kernel, cb=cb, n1=nr),
        grid=(nch // cb,),
        in_specs=[pl.BlockSpec((n1, cb * LANES), lambda i: (0, i)),
                  pl.BlockSpec((1, cb * LANES), lambda i: (0, i)),
                  full((2 * nr, n1)), full((nr, LANES)), full((nr, LANES)), full((2 * LANES, 2 * LANES))],
        out_specs=pl.BlockSpec((cb * nr, 2 * LANES), lambda i: (i, 0)),
        out_shape=jax.ShapeDtypeStruct((nch * nr, 2 * LANES), BF16),
        scratch_shapes=[pltpu.VMEM((2 * nr, cb * LANES), F32), pltpu.VMEM((cb * nr, 2 * LANES), BF16)],
        compiler_params=_cp("parallel"),
        name="hyena_spectrum",
    )(kmat, scale, tabs["frow_full"], tabs["twr"], tabs["twi"], tabs["f2"])


def _hyconv_kernel(z_ref, g_ref, ks_ref, bias_ref, frow_ref, twr_ref, twi_ref, f2_ref, f2i_ref, finv_ref,
                   o_ref, pbuf, qbuf, sbuf, *, cb, n1, sub):
    z = z_ref[...]
    _hy_forward(z.astype(BF16), frow_ref, twr_ref, twi_ref, pbuf, qbuf, cb=cb, n1=n1)
    twr, twi = twr_ref[...], twi_ref[...]
    rows = sub * n1
    for s in range(cb // sub):
        rs = slice(s * rows, (s + 1) * rows)
        zz = jnp.dot(qbuf[rs, :], f2_ref[...], preferred_element_type=F32)
        zr, zi = zz[:, 0:LANES], zz[:, LANES:]
        kr, ki = ks_ref[rs, 0:LANES].astype(F32), ks_ref[rs, LANES:2 * LANES].astype(F32)
        y = jnp.concatenate([zr * kr - zi * ki, zr * ki + zi * kr], axis=1).astype(BF16)
        rr = jnp.dot(y, f2i_ref[...], preferred_element_type=F32)
        for cc in range(sub):
            c = s * sub + cc
            r_re = rr[cc * n1:(cc + 1) * n1, 0:LANES]
            r_im = rr[cc * n1:(cc + 1) * n1, LANES:]
            sbuf[0:n1, c * LANES:(c + 1) * LANES] = (r_re * twr + r_im * twi).astype(BF16)
            sbuf[n1:2 * n1, c * LANES:(c + 1) * LANES] = (r_im * twr - r_re * twi).astype(BF16)
    y = jnp.dot(finv_ref[...], sbuf[...], preferred_element_type=F32)
    o_ref[...] = (y + z * bias_ref[...]) * g_ref[...]


def hyena_conv(z, gate, kspec, order, bias, tabs, n1, cb, sub):
    rows = z[0].shape[0]
    nch, width = HY_C, HY_C * LANES
    nr = _spec_rows(n1)
    koff = order * (nch // cb)
    full = lambda s: pl.BlockSpec(s, lambda i: (0, 0))
    io = pl.BlockSpec((rows, cb * LANES), lambda i: (0, i))
    sel = lambda a: pl.BlockSpec((rows, cb * LANES), lambda i: (0, i + a[1] // cb))
    return pl.pallas_call(
        functools.partial(_hyconv_kernel, cb=cb, n1=nr, sub=sub),
        grid=(nch // cb,),
        in_specs=[sel(z), sel(gate),
                  pl.BlockSpec((cb * nr, 2 * LANES), lambda i: (i + koff, 0)),
                  pl.BlockSpec((1, cb * LANES), lambda i: (0, i)),
                  full((2 * nr, rows)), full((nr, LANES)), full((nr, LANES)),
                  full((2 * LANES, 2 * LANES)), full((2 * LANES, 2 * LANES)), full((rows, 2 * nr))],
        out_specs=io,
        out_shape=jax.ShapeDtypeStruct((rows, width), F32),
        scratch_shapes=[pltpu.VMEM((2 * nr, cb * LANES), F32), pltpu.VMEM((cb * nr, 2 * LANES), BF16),
                        pltpu.VMEM((2 * nr, cb * LANES), BF16)],
        compiler_params=_cp("parallel"),
        name="hyena_conv",
    )(z[0], gate[0], kspec, bias, tabs["frow"], tabs["twr"], tabs["twi"], tabs["f2"], tabs["f2i"], tabs["finv"])


def _to_ab(x, rows):
    n, c = x.shape
    y = x.reshape(n // LANES, LANES, c).transpose(0, 2, 1).reshape(n // LANES, c * LANES)
    return jnp.pad(y, ((0, rows - n // LANES), (0, 0)))


def _from_ab(y, n):
    c = y.shape[1] // LANES
    return y[:n // LANES].reshape(n // LANES, c, LANES).transpose(0, 2, 1).reshape(n, c)


def hyena_mixer(xc, filt, bias):
    seq = xc.shape[0]
    if seq >= 1024:
        n1, rows, cb, sub = 2 * seq // LANES, seq // LANES, 32, 2
    else:
        n1, rows, cb, sub = 16, 16, 128, 16
    nfull = n1 * LANES
    tabs = _dft_tables(n1, rows, rows)
    tabs["frow_full"] = _dft_tables(n1, n1, rows)["frow"]
    kfull, asum = hyena_filter(seq, nfull, *filt)
    scale = jnp.repeat(1.0 / (asum[0] + EPS), LANES).reshape(1, -1)
    kspec = hyena_spectrum(_to_ab(kfull, n1), scale, tabs, n1, cb)
    xab = _to_ab(xc, rows)
    bias_l = jnp.repeat(bias.astype(F32), LANES, axis=1)
    z = hyena_conv((xab, 2 * HY_C), (xab, 0), kspec, 0, bias_l[0:1], tabs, n1, cb, sub)
    y = hyena_conv((z, 0), (xab, HY_C), kspec, 1, bias_l[1:2], tabs, n1, cb, sub)
    return _from_ab(y, seq)


def _dscan_kernel(*refs, rev, n_groups, upg, hpu, wd, use_kap):
    if use_kap:
        q_ref, k_ref, v_ref, la_ref, kap_ref, y_ref, st_ref = refs
    else:
        q_ref, k_ref, v_ref, la_ref, y_ref, st_ref = refs
    i = pl.program_id(0)

    @pl.when(i == 0)
    def _():
        st_ref[...] = jnp.zeros_like(st_ref)

    c = SCAN_CHUNK
    row = lax.broadcasted_iota(jnp.int32, (c, c), 0)
    col = lax.broadcasted_iota(jnp.int32, (c, c), 1)
    mask = (row <= col) if rev else (row >= col)
    tri = jnp.where(mask, 1.0, 0.0).astype(BF16)
    lane = lax.broadcasted_iota(jnp.int32, (c, wd), 1)
    lane1 = lax.broadcasted_iota(jnp.int32, (1, wd), 1)
    sub = wd // hpu

    def lanesel(cols, ln):
        out = cols[-1]
        for hh in range(hpu - 2, -1, -1):
            out = jnp.where(ln < (hh + 1) * sub, cols[hh], out)
        return out

    def one_chunk(rs):
        cum = _cumsum_rows(tri, la_ref[rs, :])
        cum_t = cum.T
        c_end = cum[0:1, :] if rev else cum[c - 1:c, :]
        e_all = jnp.exp(cum)
        w_all = jnp.exp(c_end - cum)
        e_end = jnp.exp(c_end)
        if use_kap:
            kap = kap_ref[rs, :]
            w_all = w_all * kap
        for g in range(n_groups):
            qg = q_ref[rs, g * LANES:(g + 1) * LANES].astype(BF16)
            kg = k_ref[rs, g * LANES:(g + 1) * LANES]
            gm = lax.dot_general(qg, kg.astype(BF16), (((1,), (1,)), ((), ())), preferred_element_type=F32)
            kg_t = kg.T.astype(BF16)
            for uu in range(upg):
                u = g * upg + uu
                v = v_ref[rs, u * wd:(u + 1) * wd]
                heads = [u * hpu + hh for hh in range(hpu)]
                st = st_ref[u]
                y = lanesel([e_all[:, h:h + 1] for h in heads], lane) * jnp.dot(
                    qg, st.astype(BF16), preferred_element_type=F32)
                for hh, h in enumerate(heads):
                    dmat = jnp.exp(jnp.where(mask, cum[:, h:h + 1] - cum_t[h:h + 1, :], NEG))
                    vh = v * kap[:, h:h + 1] if use_kap else v
                    if hpu > 1:
                        vh = jnp.where((lane >= hh * sub) & (lane < (hh + 1) * sub), vh, 0.0)
                    y = y + jnp.dot((gm * dmat).astype(BF16), vh.astype(BF16), preferred_element_type=F32)
                y_ref[rs, u * wd:(u + 1) * wd] = y
                vw = v * lanesel([w_all[:, h:h + 1] for h in heads], lane)
                st_ref[u] = (lanesel([e_end[:, h:h + 1] for h in heads], lane1) * st
                             + jnp.dot(kg_t, vw.astype(BF16), preferred_element_type=F32))

    n_sub = q_ref.shape[0] // c
    for s in (reversed(range(n_sub)) if rev else range(n_sub)):
        one_chunk(slice(s * c, (s + 1) * c))


def _chunk_order(n, n_lat, c, rev):
    nl, nt = n_lat // c, n // c
    if rev:
        return lambda i: nt - 1 - i
    return lambda i: jnp.where(i < nt - nl, nl + i, i - (nt - nl))


def decay_scan(q, k, v, la, kap, n_lat, *, rev, n_groups, upg, hpu, wd):
    n = q[0].shape[0]
    c = SCAN_STEP_CHUNKS * SCAN_CHUNK
    n_units = n_groups * upg
    rmap = _chunk_order(n, n_lat, c, rev)

    def spec(width, blk):
        return pl.BlockSpec((c, width), lambda i: (rmap(i), blk))

    ins = [q, k, v, la] + ([kap] if kap is not None else [])
    widths = [n_groups * LANES, n_groups * LANES, n_units * wd, LANES] + ([LANES] if kap is not None else [])
    return pl.pallas_call(
        functools.partial(_dscan_kernel, rev=rev, n_groups=n_groups, upg=upg, hpu=hpu, wd=wd,
                          use_kap=kap is not None),
        grid=(n // c,),
        in_specs=[spec(w, a[1]) for w, a in zip(widths, ins)],
        out_specs=pl.BlockSpec((c, n_units * wd), lambda i: (rmap(i), 0)),
        out_shape=jax.ShapeDtypeStruct((n, n_units * wd), F32),
        scratch_shapes=[pltpu.VMEM((n_units, LANES, wd), F32)],
        compiler_params=_cp("arbitrary"),
        name="decay_scan_rev" if rev else "decay_scan_fwd",
    )(*[a[0] for a in ins])


def _post_kernel(*refs, n_terms, skip, pre_gate, ngroups):
    terms = refs[:n_terms]
    pos = n_terms
    y = terms[0][...]
    for t in terms[1:]:
        y = y + t[...]
    if skip:
        y = y + refs[pos][...] * refs[pos + 1][...]
        pos += 2
    gate_ref, ng_ref, o_ref = refs[pos:pos + 3]
    gate = _silu(gate_ref[...])
    if pre_gate:
        y = y * gate
    gw = y.shape[1] // ngroups
    for g in range(ngroups):
        sl = slice(g * gw, (g + 1) * gw)
        yg = y[:, sl]
        o = yg * lax.rsqrt(jnp.mean(yg * yg, axis=-1, keepdims=True) + EPS) * ng_ref[:, sl]
        if not pre_gate:
            o = o * gate[:, sl]
        o_ref[:, sl] = o.astype(o_ref.dtype)


def mixer_post(terms, skip, gate, norm_g, *, pre_gate, ngroups, width=1024, tm=256):
    n = terms[0][0].shape[0]
    tm = min(tm, n)
    blk = lambda a: pl.BlockSpec((tm, width), lambda i: (i, a[1]))
    vec = pl.BlockSpec((1, width), lambda i: (0, 0))
    arrays = [t[0] for t in terms]
    specs = [blk(t) for t in terms]
    if skip is not None:
        arrays += [skip[0][0], skip[1].reshape(1, width)]
        specs += [blk(skip[0]), vec]
    arrays += [gate[0], norm_g.reshape(1, width)]
    specs += [blk(gate), vec]
    return pl.pallas_call(
        functools.partial(_post_kernel, n_terms=len(terms), skip=skip is not None, pre_gate=pre_gate,
                          ngroups=ngroups),
        grid=(n // tm,),
        in_specs=specs,
        out_specs=pl.BlockSpec((tm, width), lambda i: (i, 0)),
        out_shape=jax.ShapeDtypeStruct((n, width), BF16),
        compiler_params=_cp("parallel"),
        name="mixer_post",
    )(*arrays)


def _ssd_dt_kernel(x_ref, b_ref, a_ref, dt_ref, la_ref):
    x = x_ref[...] + b_ref[...]
    dt = jnp.maximum(x, 0.0) + jnp.log1p(jnp.exp(-jnp.abs(x)))
    dt_ref[...] = dt
    la_ref[...] = dt * a_ref[...]


def ssd_mixer(u, dt_raw, conv_w, conv_b, dt_bias, a_log, d_skip, norm_g, n_lat):
    n = u.shape[0]
    xbc = conv3(u, conv_w, conv_b, True, starts=(0, n_lat), col0=SSD_INNER)
    pad = lambda t: jnp.pad(t.astype(F32), ((0, 0), (0, LANES - SSD_HEADS))).reshape(1, 2 * LANES)
    tm = 256
    dt, la = pl.pallas_call(
        _ssd_dt_kernel,
        grid=(n // tm,),
        in_specs=[pl.BlockSpec((tm, 2 * LANES), lambda i: (i, 0)),
                  pl.BlockSpec((1, 2 * LANES), lambda i: (0, 0)),
                  pl.BlockSpec((1, 2 * LANES), lambda i: (0, 0))],
        out_specs=[pl.BlockSpec((tm, 2 * LANES), lambda i: (i, 0))] * 2,
        out_shape=[jax.ShapeDtypeStruct((n, 2 * LANES), F32)] * 2,
        compiler_params=_cp("parallel"),
        name="ssd_dt",
    )(dt_raw, pad(dt_bias), pad(-jnp.exp(a_log.astype(F32))))
    ys = [(decay_scan((xbc, 3), (xbc, 2), (xbc, 0), (la, d), (dt, d), n_lat,
                      rev=bool(d), n_groups=SSD_GROUPS, upg=2, hpu=2, wd=LANES), 0) for d in range(2)]
    dsk = jnp.repeat(d_skip.astype(F32), SSD_P)
    return mixer_post(ys, ((xbc, 0), dsk), (u, 0), norm_g, pre_gate=True, ngroups=SSD_GROUPS)


def _rope_kernel(q_ref, k_ref, inv_ref, qo_ref, ko_ref, *, n_lat, n_ctx):
    tm = q_ref.shape[0]
    row = pl.program_id(0) * tm + lax.broadcasted_iota(jnp.int32, (tm, 1), 0)
    pos = jnp.where(row < n_lat, row + n_ctx, row - n_lat).astype(F32)
    ang = pos * inv_ref[...]
    cs, sn = jnp.cos(ang), jnp.sin(ang)
    lane = lax.broadcasted_iota(jnp.int32, (tm, LANES), 1)
    sgn = jnp.where(lane < RET_DK // 2, -sn, sn)
    for h in range(RET_HEADS):
        sl = slice(h * RET_DK, (h + 1) * RET_DK)
        for src, dst, scale in ((q_ref, qo_ref, 1.0), (k_ref, ko_ref, RET_DK ** -0.5)):
            x = src[:, sl]
            y = x * cs + pltpu.roll(x, RET_DK // 2, 1) * sgn
            dst[:, sl] = y * scale


def retention_mixer(u, norm_g, n_lat):
    n = u.shape[0]
    tm = 256
    half = RET_DK // 2
    inv = ROPE_BASE ** (-jnp.arange(half, dtype=F32) / half)
    inv2 = jnp.concatenate([inv, inv]).reshape(1, LANES)
    wq = RET_HEADS * RET_DK
    qr, kr = pl.pallas_call(
        functools.partial(_rope_kernel, n_lat=n_lat, n_ctx=n - n_lat),
        grid=(n // tm,),
        in_specs=[pl.BlockSpec((tm, wq), lambda i: (i, 0)), pl.BlockSpec((tm, wq), lambda i: (i, 1)),
                  pl.BlockSpec((1, LANES), lambda i: (0, 0))],
        out_specs=[pl.BlockSpec((tm, wq), lambda i: (i, 0))] * 2,
        out_shape=[jax.ShapeDtypeStruct((n, wq), F32)] * 2,
        compiler_params=_cp("parallel"),
        name="rope",
    )(u, u, inv2)
    log_gamma = jnp.log1p(-jnp.exp2(-5.0 - jnp.arange(RET_HEADS, dtype=F32)))
    la = jnp.broadcast_to(jnp.pad(log_gamma, (0, LANES - RET_HEADS)).reshape(1, LANES), (n, LANES))
    ys = [(decay_scan((qr, 0), (kr, 0), (u, 1), (la, 0), None, n_lat,
                      rev=bool(d), n_groups=RET_HEADS, upg=1, hpu=1, wd=RET_DV), 0) for d in range(2)]
    return mixer_post(ys, None, (u, 2), norm_g, pre_gate=False, ngroups=RET_HEADS)


def _anchor_rows(gc, b, rev):
    c, n = gc.shape
    shift = b if rev else b - 1
    if 2 * b >= 8:
        return jnp.concatenate([jnp.broadcast_to(gc[s + shift:s + shift + 1, :], (2 * b, n))
                                for s in range(0, c, 2 * b)], axis=0)
    sub = lax.broadcasted_iota(jnp.int32, (8, n), 0)
    tiles = []
    for t in range(0, c, 8):
        tile = None
        for m in range(8 // (2 * b)):
            r = t + m * 2 * b + shift
            cand = jnp.broadcast_to(gc[r:r + 1, :], (8, n))
            tile = cand if tile is None else jnp.where(sub >= m * 2 * b, cand, tile)
        tiles.append(tile)
    return jnp.concatenate(tiles, axis=0)


def _hg_kernel(q_ref, f_ref, v_ref, lb_ref, y_ref, st_ref, *, rev):
    @pl.when(pl.program_id(0) == 0)
    def _():
        st_ref[...] = jnp.zeros_like(st_ref)

    n_sub = q_ref.shape[0] // HG_CHUNK
    for s in (reversed(range(n_sub)) if rev else range(n_sub)):
        _hg_chunk(q_ref, f_ref, v_ref, lb_ref, y_ref, st_ref, slice(s * HG_CHUNK, (s + 1) * HG_CHUNK), rev)


def _hg_chunk(q_ref, f_ref, v_ref, lb_ref, y_ref, st_ref, rs, rev):
    c, width = HG_CHUNK, q_ref.shape[1]
    nh = width // HG_DK
    lb = lb_ref[...]
    qh = _silu(q_ref[rs, :])
    fr = f_ref[rs, :]
    lg = jnp.log(lb + (1.0 - lb) * jax.nn.sigmoid(fr))
    kk = (1.0 - lb) * jax.nn.sigmoid(-fr)
    row = lax.broadcasted_iota(jnp.int32, (c, c), 0)
    col = lax.broadcasted_iota(jnp.int32, (c, c), 1)
    tri = jnp.where((row <= col) if rev else (row >= col), 1.0, 0.0).astype(BF16)
    gc = _cumsum_rows(tri, lg)
    g_end = gc[0:1, :] if rev else gc[c - 1:c, :]
    qe = (qh * jnp.exp(gc)).astype(BF16)
    kw = (kk * jnp.exp(g_end - gc)).astype(BF16)
    dec = jnp.exp(g_end)
    vb = v_ref[rs, :].astype(BF16)
    qb, kb = qh.astype(BF16), kk.astype(BF16)
    nt = (((1,), (1,)), ((), ()))
    hs = [slice(h * HG_DK, (h + 1) * HG_DK) for h in range(nh)]
    scores = [jnp.where(row == col, lax.dot_general(qb[:, s], kb[:, s], nt, preferred_element_type=F32), 0.0)
              for s in hs]
    ridx = lax.broadcasted_iota(jnp.int32, (c, 1), 0)
    lvl = 0
    while (1 << lvl) < c:
        b = 1 << lvl
        anchor = _anchor_rows(gc, b, rev)
        upper = ((ridx >> lvl) & 1) == 1
        q_on, k_on = (~upper, upper) if rev else (upper, ~upper)
        qt = (qh * jnp.exp(jnp.where(q_on, gc - anchor, NEG))).astype(BF16)
        kt = (kk * jnp.exp(jnp.where(k_on, anchor - gc, NEG))).astype(BF16)
        same = (row >> (lvl + 1)) == (col >> (lvl + 1))
        for h, s in enumerate(hs):
            sc = lax.dot_general(qt[:, s], kt[:, s], nt, preferred_element_type=F32)
            scores[h] = scores[h] + jnp.where(same, sc, 0.0)
        lvl += 1
    for h, s in enumerate(hs):
        st = st_ref[h]
        y = jnp.dot(scores[h].astype(BF16), vb[:, s], preferred_element_type=F32)
        y = y + lax.dot_general(qe[:, s], st.astype(BF16), nt, preferred_element_type=F32)
        y_ref[rs, s] = y
        st_ref[h] = dec[:, s] * st + lax.dot_general(vb[:, s], kw[:, s], (((0,), (0,)), ((), ())),
                                                     preferred_element_type=F32)


def hgrn2_scan(u, lb, n_lat, *, rev, d):
    n = u.shape[0]
    c = SCAN_STEP_CHUNKS * HG_CHUNK
    rmap = _chunk_order(n, n_lat, c, rev)
    spec = lambda blk: pl.BlockSpec((c, HG_W), lambda i: (rmap(i), blk))
    return pl.pallas_call(
        functools.partial(_hg_kernel, rev=rev),
        grid=(n // c,),
        in_specs=[spec(0), spec(1 + d), spec(3), pl.BlockSpec((1, HG_W), lambda i: (0, 0))],
        out_specs=pl.BlockSpec((c, HG_W), lambda i: (rmap(i), 0)),
        out_shape=jax.ShapeDtypeStruct((n, HG_W), F32),
        scratch_shapes=[pltpu.VMEM((HG_HEADS, HG_DK, HG_DK), F32)],
        compiler_params=_cp("arbitrary"),
        name="hgrn2_scan_rev" if rev else "hgrn2_scan_fwd",
    )(u, u, u, lb)


def hgrn2_mixer(u, lb, norm_g, n_lat):
    ys = [(hgrn2_scan(u, lb[d:d + 1].astype(F32), n_lat, rev=bool(d), d=d), 0) for d in range(2)]
    return mixer_post(ys, None, (u, 4), norm_g, pre_gate=False, ngroups=HG_HEADS)


_HIGH16 = 0xFFFF0000


def _pack_rows(y, o_ref):
    t, d = y.shape
    nj = d // (2 * LANES)
    words = []
    for j in range(nj):
        lo = y[:, 2 * j * LANES:(2 * j + 1) * LANES].astype(BF16).astype(F32)
        hi = y[:, (2 * j + 1) * LANES:(2 * j + 2) * LANES].astype(BF16).astype(F32)
        words.append((pltpu.bitcast(lo, jnp.uint32) >> 16) | (pltpu.bitcast(hi, jnp.uint32) & jnp.uint32(_HIGH16)))
    o_ref[...] = jnp.swapaxes(jnp.stack(words, axis=0), 0, 1).reshape(t * nj, LANES)


def _unpack_rows(p_ref, t, nj):
    return jnp.swapaxes(p_ref[...].reshape(t, nj, LANES), 0, 1)


def _unpack_piece(words, j):
    lo = pltpu.bitcast(words[j] << 16, F32)
    hi = pltpu.bitcast(words[j] & jnp.uint32(_HIGH16), F32)
    return lo, hi


def _router_kernel(x_ref, g_ref, sc_ref, sh_ref, wr_ref, br_ref, h_ref, r_ref, *, n_lat):
    x = x_ref[...]
    tm = x.shape[0]
    y = x * lax.rsqrt(jnp.mean(x * x, axis=-1, keepdims=True) + EPS) * g_ref[...]
    h = y * (1.0 + _row_select(sc_ref, tm, n_lat)) + _row_select(sh_ref, tm, n_lat)
    _pack_rows(h, h_ref)
    h_hi = h.astype(BF16)
    h_lo = (h - h_hi.astype(F32)).astype(BF16)
    lg = (jnp.dot(h_hi, wr_ref[0], preferred_element_type=F32) + jnp.dot(h_hi, wr_ref[1], preferred_element_type=F32)
          + jnp.dot(h_lo, wr_ref[0], preferred_element_type=F32) + br_ref[...])
    lane = lax.broadcasted_iota(jnp.int32, lg.shape, 1).astype(F32)
    big = float(LANES)
    is_g = lane < MOE_GROUPS
    gmax = jnp.max(jnp.where(is_g, lg, -jnp.inf), axis=1, keepdims=True)
    gsel = jnp.min(jnp.where(is_g & (lg == gmax), lane, big), axis=1, keepdims=True)
    pg = 1.0 / jnp.sum(jnp.where(is_g, jnp.exp(lg - gmax), 0.0), axis=1, keepdims=True)
    lo = MOE_GROUPS + MOE_PER_GROUP * gsel
    in_e = (lane >= lo) & (lane < lo + MOE_PER_GROUP)
    v1 = jnp.max(jnp.where(in_e, lg, -jnp.inf), axis=1, keepdims=True)
    i1 = jnp.min(jnp.where(in_e & (lg == v1), lane, big), axis=1, keepdims=True)
    rest = in_e & (lane != i1)
    v2 = jnp.max(jnp.where(rest, lg, -jnp.inf), axis=1, keepdims=True)
    i2 = jnp.min(jnp.where(rest & (lg == v2), lane, big), axis=1, keepdims=True)
    t = jnp.exp(v2 - v1)
    w1 = pg / (1.0 + t)
    w2 = pg * t / (1.0 + t)
    r_ref[...] = jnp.where(lane == 0, i1 - MOE_GROUPS,
                           jnp.where(lane == 1, i2 - MOE_GROUPS,
                                     jnp.where(lane == 2, w1, jnp.where(lane == 3, w2, 0.0))))


def moe_router(x, g, sc, sh, wr, br, m, n_lat, tm=256):
    d = x.shape[1]
    nj = d // (2 * LANES)
    vec = pl.BlockSpec((1, d), lambda i: (0, 0))
    two = pl.BlockSpec((2, d), lambda i: (0, 0))
    return pl.pallas_call(
        functools.partial(_router_kernel, n_lat=n_lat),
        grid=(m // tm,),
        in_specs=[pl.BlockSpec((tm, d), lambda i: (i, 0)), vec, two, two,
                  pl.BlockSpec((2, d, LANES), lambda i: (0, 0, 0)), pl.BlockSpec((1, LANES), lambda i: (0, 0))],
        out_specs=[pl.BlockSpec((tm * nj, LANES), lambda i: (i, 0)), pl.BlockSpec((tm, LANES), lambda i: (i, 0))],
        out_shape=[jax.ShapeDtypeStruct((m * nj, LANES), jnp.uint32), jax.ShapeDtypeStruct((m, LANES), F32)],
        compiler_params=_cp("parallel"),
        name="moe_router",
    )(x, g.reshape(1, d), sc, sh, wr, br)


def _moe_plan(route, m, n_tiles):
    t = MOE_TILE
    pe = jnp.concatenate([route[:, 0], route[:, 1]]).astype(jnp.int32)
    onehot = (pe[:, None] == jnp.arange(MOE_EXPERTS, dtype=jnp.int32)[None, :]).astype(jnp.int32)
    csum = jnp.cumsum(onehot, axis=0)
    rank = jnp.take_along_axis(csum, pe[:, None], axis=1)[:, 0] - 1
    cnt = csum[-1]
    ntile = (cnt + t - 1) // t
    tend = jnp.cumsum(ntile)
    tstart = tend - ntile
    slot = tstart[pe] * t + rank
    tid = jnp.arange(n_tiles, dtype=jnp.int32)
    tile_e = jnp.clip(jnp.searchsorted(tend, tid, side="right"), 0, MOE_EXPERTS - 1).astype(jnp.int32)
    tile_valid = jnp.where(tid < tend[-1], jnp.clip(cnt[tile_e] - (tid - tstart[tile_e]) * t, 0, t), 0)
    tok = jnp.tile(jnp.arange(m, dtype=jnp.int32), 2)
    row_tok = jnp.zeros((n_tiles * t,), jnp.int32).at[slot].set(tok)
    return row_tok, tile_e, tile_valid.astype(jnp.int32), slot[:m], slot[m:]


def _row_copies(idx_ref, base, src_hbm, dst, sem, rows, nj, wait):
    if wait:
        pltpu.make_async_copy(src_hbm.at[pl.ds(0, rows * nj), :], dst, sem).wait()
        return

    def body(r, carry):
        src = idx_ref[base + r] * nj
        pltpu.make_async_copy(src_hbm.at[pl.ds(src, nj), :], dst.at[pl.ds(r * nj, nj), :], sem).start()
        return carry
    lax.fori_loop(0, rows, body, 0, unroll=8)


def _expert_kernel(te_ref, tv_ref, tok_ref, hp_hbm, w1_ref, w3_ref, w2_ref, o_ref, buf, xbuf, sem, *, nj):
    i = pl.program_id(0)
    n = pl.num_programs(0)
    t = xbuf.shape[0]

    def fetch(tile, slot):
        _row_copies(tok_ref, tile * t, hp_hbm, buf.at[slot], sem.at[slot], t, nj, wait=False)

    @pl.when((i == 0) & (tv_ref[0] > 0))
    def _():
        fetch(0, 0)

    @pl.when((i + 1 < n) & (tv_ref[jnp.minimum(i + 1, n - 1)] > 0))
    def _():
        fetch(i + 1, (i + 1) % 2)

    @pl.when(tv_ref[i] > 0)
    def _():
        slot = i % 2
        cur = buf.at[slot]
        _row_copies(tok_ref, 0, hp_hbm, cur, sem.at[slot], t, nj, wait=True)
        words = _unpack_rows(cur, t, nj)
        for j in range(nj):
            lo, hi = _unpack_piece(words, j)
            xbuf[:, 2 * j * LANES:(2 * j + 1) * LANES] = lo.astype(BF16)
            xbuf[:, (2 * j + 1) * LANES:(2 * j + 2) * LANES] = hi.astype(BF16)
        x = xbuf[...]
        a = jnp.dot(x, w1_ref[0, 0], preferred_element_type=F32)
        b = jnp.dot(x, w3_ref[0, 0], preferred_element_type=F32)
        y = jnp.dot((_silu(a) * b).astype(BF16), w2_ref[0, 0], preferred_element_type=F32)
        _pack_rows(y, o_ref)

    @pl.when(tv_ref[i] == 0)
    def _():
        o_ref[...] = jnp.zeros_like(o_ref)


def moe_experts(hp, row_tok, tile_e, tile_valid, w1, w3, w2, layer):
    t = MOE_TILE
    n_tiles = tile_e.shape[0]
    d, ff = w1.shape[2], w1.shape[3]
    nj = d // (2 * LANES)
    wmap = lambda i, te, tv, tok: (layer, te[i], 0, 0)
    return pl.pallas_call(
        functools.partial(_expert_kernel, nj=nj),
        grid_spec=pltpu.PrefetchScalarGridSpec(
            num_scalar_prefetch=3, grid=(n_tiles,),
            in_specs=[pl.BlockSpec(memory_space=pl.ANY),
                      pl.BlockSpec((1, 1, d, ff), wmap), pl.BlockSpec((1, 1, d, ff), wmap),
                      pl.BlockSpec((1, 1, ff, d), wmap)],
            out_specs=pl.BlockSpec((t * nj, LANES), lambda i, te, tv, tok: (i, 0)),
            scratch_shapes=[pltpu.VMEM((2, t * nj, LANES), jnp.uint32), pltpu.VMEM((t, d), BF16),
                            pltpu.SemaphoreType.DMA((2,))]),
        out_shape=jax.ShapeDtypeStruct((n_tiles * t * nj, LANES), jnp.uint32),
        compiler_params=_cp("arbitrary"),
        name="moe_experts",
    )(tile_e, tile_valid, row_tok, hp, w1, w3, w2)


def _combine_kernel(s0_ref, s1_ref, x_ref, r_ref, g_ref, fg_ref, ys_hbm, o_ref, buf, sem, *, n_lat, final, nj):
    tt = x_ref.shape[0]
    i = pl.program_id(0)
    n = pl.num_programs(0)
    base = i * tt

    def fetch(step, slot):
        for k, s_ref in enumerate((s0_ref, s1_ref)):
            _row_copies(s_ref, step * tt, ys_hbm, buf.at[slot, k], sem.at[slot, k], tt, nj, wait=False)

    @pl.when(i == 0)
    def _():
        fetch(0, 0)

    @pl.when(i + 1 < n)
    def _():
        fetch(i + 1, (i + 1) % 2)

    slot = i % 2
    for k in range(2):
        _row_copies(s0_ref, 0, ys_hbm, buf.at[slot, k], sem.at[slot, k], tt, nj, wait=True)
    r = r_ref[...]
    w0 = jnp.broadcast_to(r[:, 2:3], (tt, LANES))
    w1 = jnp.broadcast_to(r[:, 3:4], (tt, LANES))
    is_ctx = base >= n_lat
    words0 = _unpack_rows(buf.at[slot, 0], tt, nj)
    words1 = _unpack_rows(buf.at[slot, 1], tt, nj)
    for j in range(nj):
        p0 = _unpack_piece(words0, j)
        p1 = _unpack_piece(words1, j)
        for half in range(2):
            cs = slice((2 * j + half) * LANES, (2 * j + half + 1) * LANES)
            gate = jnp.where(is_ctx, g_ref[1:2, cs], g_ref[0:1, cs])
            o_ref[:, cs] = x_ref[:, cs] + gate * (w0 * p0[half] + w1 * p1[half])
    if final:
        y = o_ref[...]
        o_ref[...] = y * lax.rsqrt(jnp.mean(y * y, axis=-1, keepdims=True) + EPS) * fg_ref[...]


def moe_combine(x, route, gate, final_g, ys, slot0, slot1, m, n_lat, final, tt=256):
    d = x.shape[1]
    nj = d // (2 * LANES)
    assert n_lat % tt == 0
    return pl.pallas_call(
        functools.partial(_combine_kernel, n_lat=n_lat, final=final, nj=nj),
        grid_spec=pltpu.PrefetchScalarGridSpec(
            num_scalar_prefetch=2, grid=(m // tt,),
            in_specs=[pl.BlockSpec((tt, d), lambda i, a, b: (i, 0)),
                      pl.BlockSpec((tt, LANES), lambda i, a, b: (i, 0)),
                      pl.BlockSpec((2, d), lambda i, a, b: (0, 0)),
                      pl.BlockSpec((1, d), lambda i, a, b: (0, 0)),
                      pl.BlockSpec(memory_space=pl.ANY)],
            out_specs=pl.BlockSpec((tt, d), lambda i, a, b: (i, 0)),
            scratch_shapes=[pltpu.VMEM((2, 2, tt * nj, LANES), jnp.uint32), pltpu.SemaphoreType.DMA((2, 2))]),
        out_shape=jax.ShapeDtypeStruct((m, d), F32),
        compiler_params=_cp("arbitrary"),
        name="moe_combine",
    )(slot0, slot1, x, route, gate, final_g.reshape(1, d), ys)


def hier_moe(x, norm_g, sc, sh, gate, P, final_g, m, n_lat, final):
    n_tiles = 2 * m // MOE_TILE + MOE_EXPERTS
    hp, route = moe_router(x, norm_g, sc, sh, P["moe_wr"], P["moe_br"], m, n_lat)
    row_tok, tile_e, tile_valid, slot0, slot1 = _moe_plan(route, m, n_tiles)
    ys = moe_experts(hp, row_tok, tile_e, tile_valid, P["moe_w1"], P["moe_w3"], P["moe_w2"], P["layer"])
    return moe_combine(x, route, gate, final_g, ys, slot0, slot1, m, n_lat, final)


def _to_cols(t, rows):
    n, ch = t.shape
    return jnp.swapaxes(t.reshape(rows, GRID_W, ch), 0, 1).reshape(n, ch)


def _from_cols(t, rows):
    n, ch = t.shape
    return jnp.swapaxes(t.reshape(GRID_W, rows, ch), 0, 1).reshape(n, ch)


def _layer(xa, mod, P, lb, n_lat, last, final_g):
    n, d = xa.shape
    sh1, sc1, g1, sh2, sc2, g2 = (mod[:, q * d:(q + 1) * d] for q in range(6))
    grid_rows = n_lat // GRID_W
    h, h_lat = norm_modulate(xa, P["norm1"], sc1, sh1, n_lat)
    h_cols = jnp.concatenate([_to_cols(h_lat, grid_rows), h[n_lat:]], axis=0)
    tm = 768 if n % 768 == 0 else 256
    layer, w_in = P["layer"], P["w_in"]
    u_hy = matmul(h, w_in["hy"], layer, 0, 3072, tm)
    u_ssd = matmul(h, w_in["ssd"], layer, 0, 3072, tm)
    u_hg = matmul(h_cols, w_in["hg"], layer, 0, 5120, tm)
    u_ret = matmul(h, w_in["ret"], layer, 0, 3072, tm)
    dt_raw = matmul(h, P["w_dt"], layer, 0, 2 * LANES, tm, tn=2 * LANES)
    filt = (P["hy_w1"], P["hy_b1"], P["hy_fr1"], P["hy_w2"], P["hy_b2"], P["hy_fr2"], P["hy_w3"])
    xc_lat, xc_ctx = conv3(u_hy, P["hy_conv_w"], P["hy_conv_b"], False, starts=(0, n_lat), split=True)
    a = hyena_mixer(xc_lat, filt, P["hy_bias"]).astype(BF16)
    if not last:
        a = jnp.concatenate([a, hyena_mixer(xc_ctx, filt, P["hy_bias"]).astype(BF16)], axis=0)
    b = ssd_mixer(u_ssd, dt_raw, P["ssd_conv_w"], P["ssd_conv_b"], P["ssd_dt_bias"], P["ssd_a_log"],
                  P["ssd_d"], P["ssd_norm_g"], n_lat)
    c = hgrn2_mixer(u_hg, lb, P["hg_norm_g"], n_lat)
    c = jnp.concatenate([_from_cols(c[:n_lat], grid_rows), c[n_lat:]], axis=0)
    dd = retention_mixer(u_ret, P["ret_norm_g"], n_lat)
    m = n_lat if last else n
    xa = matmul_out((a, b, c, dd), P["w_out"], layer, xa, g1, m, n_lat, tm=1024 if m % 1024 == 0 else tm)
    return hier_moe(xa, P["norm2"], sc2, sh2, g2, P, final_g, m, n_lat, last)


def kernel(x, c, ctx, c_ctx, ada_w, ada_b, norm1_g, norm2_g, w_in, w_out, hy_conv_w, hy_conv_b, hy_w1, hy_b1, hy_fr1, hy_w2, hy_b2, hy_fr2, hy_w3, hy_bias, ssd_conv_w, ssd_conv_b, ssd_dt_bias, ssd_a_log, ssd_d, ssd_norm_g, hg_lb_raw, hg_norm_g, ret_norm_g, moe_wg, moe_bg, moe_we, moe_be, moe_w1, moe_w3, moe_w2, final_g):
    depth = ada_w.shape[0]
    n_lat = x.shape[1]
    lb_prob = jax.nn.softmax(hg_lb_raw.astype(F32), axis=0)
    lb_all = jnp.cumsum(lb_prob, axis=0) - lb_prob[0]
    mods = ada_modulation(c, c_ctx, ada_w, ada_b)
    xa = jnp.concatenate([x[0], ctx[0]], axis=0)
    o_ssd, o_dt, o_hg, o_ret = 3072, 6144, 6176, 11296
    w_groups = {"hy": w_in[:, :, :o_ssd].astype(BF16), "ssd": w_in[:, :, o_ssd:o_dt].astype(BF16),
                "hg": w_in[:, :, o_hg:o_ret].astype(BF16), "ret": w_in[:, :, o_ret:].astype(BF16)}
    zpad = jnp.zeros(w_in.shape[:2] + (LANES - SSD_HEADS,), w_in.dtype)
    w_dt_b = jnp.concatenate([w_in[:, :, o_dt:o_dt + SSD_HEADS], zpad,
                              w_in[:, :, o_dt + SSD_HEADS:o_hg], zpad], axis=2).astype(BF16)
    w_out_b = w_out.astype(BF16)
    moe_b = [w.astype(BF16) for w in (moe_w1, moe_w3, moe_w2)]
    for l in range(depth):
        pad_r = LANES - MOE_GROUPS - MOE_EXPERTS
        wr = jnp.pad(jnp.concatenate([moe_wg[l], moe_we[l]], axis=1).astype(F32), ((0, 0), (0, pad_r)))
        wr_hi = wr.astype(BF16)
        wr_lo = (wr - wr_hi.astype(F32)).astype(BF16)
        P = {
            "layer": l, "norm1": norm1_g[l], "norm2": norm2_g[l], "w_in": w_groups, "w_dt": w_dt_b, "w_out": w_out_b,
            "hy_conv_w": hy_conv_w[l], "hy_conv_b": hy_conv_b[l], "hy_w1": hy_w1[l], "hy_b1": hy_b1[l],
            "hy_fr1": hy_fr1[l], "hy_w2": hy_w2[l], "hy_b2": hy_b2[l], "hy_fr2": hy_fr2[l], "hy_w3": hy_w3[l],
            "hy_bias": hy_bias[l],
            "ssd_conv_w": ssd_conv_w[l], "ssd_conv_b": ssd_conv_b[l], "ssd_dt_bias": ssd_dt_bias[l],
            "ssd_a_log": ssd_a_log[l], "ssd_d": ssd_d[l], "ssd_norm_g": ssd_norm_g[l],
            "hg_norm_g": hg_norm_g[l], "ret_norm_g": ret_norm_g[l],
            "moe_wr": jnp.stack([wr_hi, wr_lo]),
            "moe_br": jnp.pad(jnp.concatenate([moe_bg[l], moe_be[l]]).astype(F32), (0, pad_r)).reshape(1, LANES),
            "moe_w1": moe_b[0], "moe_w3": moe_b[1], "moe_w2": moe_b[2],
        }
        xa = _layer(xa, mods[l, 0:2], P, lb_all[l], n_lat, l == depth - 1, final_g)
    return xa[None]
```
